```python
import math
import jax, jax.numpy as jnp
from jax import lax
import numpy as np

D_MODEL = 1024
BATCH = 8
SEQ = 4096
DEPTH = 2

N_MIXERS = 2
N_ATTN_LAYERS = (DEPTH + 1) // 2
N_HGRN_LAYERS = DEPTH // 2

ATT_HEADS = 16
ATT_HEAD_DIM = 64
ATT_KV_HEADS = 2
Q_LORA_RANK = 256
IDX_HEADS = 8
IDX_DIM = 128
INDEX_TOPK = 256
Q_BLOCK = 128
ATT_KV_DIM = ATT_KV_HEADS * ATT_HEAD_DIM
ATT_IN_DIM = Q_LORA_RANK + 2 * ATT_KV_DIM + IDX_DIM + IDX_HEADS

HGRN_EXPAND = 128
HGRN_HEADS = D_MODEL // HGRN_EXPAND
HGRN_F_DIM = HGRN_HEADS * HGRN_EXPAND
HGRN_V_DIM = D_MODEL // HGRN_HEADS
HGRN_CHUNK = 64
HGRN_IN_DIM = 2 * HGRN_F_DIM + D_MODEL + D_MODEL

D_FF = -(-8 * D_MODEL // (3 * 256)) * 256

ROPE_THETA = 500000.0
ATT_ROT_DIM = ATT_HEAD_DIM // 4
IDX_ROT_DIM = IDX_DIM // 4
DEEPNORM_ALPHA = (2 * DEPTH) ** 0.25
DEEPNORM_BETA = (8 * DEPTH) ** -0.25
LN_EPS = 1e-5
RMS_EPS = 1e-6

kernel_name = "dsa_hgrn2_interleaved_deepnorm"


def layer_norm(x, g, b):
    xf = x.astype(jnp.float32)
    mu = jnp.mean(xf, axis=-1, keepdims=True)
    var = jnp.mean(jnp.square(xf - mu), axis=-1, keepdims=True)
    return ((xf - mu) * lax.rsqrt(var + LN_EPS) * g.astype(jnp.float32) + b.astype(jnp.float32)).astype(x.dtype)


def rms_norm(x, g):
    xf = x.astype(jnp.float32)
    ms = jnp.mean(jnp.square(xf), axis=-1, keepdims=True)
    return (xf * lax.rsqrt(ms + RMS_EPS) * g.astype(jnp.float32)).astype(x.dtype)


def rope_tables(positions, rot_dim):
    inv = ROPE_THETA ** (-jnp.arange(0, rot_dim, 2, dtype=jnp.float32) / rot_dim)
    ang = positions.astype(jnp.float32)[..., None] * inv
    return jnp.cos(ang), jnp.sin(ang)


def apply_rope(x, cos, sin):
    half = cos.shape[-1]
    cos = cos.astype(x.dtype)
    sin = sin.astype(x.dtype)
    x1 = x[..., :half]
    x2 = x[..., half:2 * half]
    return jnp.concatenate([x1 * cos - x2 * sin, x2 * cos + x1 * sin, x[..., 2 * half:]], axis=-1)


def dsa_mixer(x, cos_q, sin_q, cos_i, sin_i, w_in, g_cq, w_uq, w_iq, g_ik, b_ik, w_o):
    B, T, _ = x.shape
    dt = x.dtype
    h = x @ w_in
    c_q, k, v, k_idx, w_idx = jnp.split(
        h, [Q_LORA_RANK, Q_LORA_RANK + ATT_KV_DIM, Q_LORA_RANK + 2 * ATT_KV_DIM,
            Q_LORA_RANK + 2 * ATT_KV_DIM + IDX_DIM], axis=-1)
    c_q = rms_norm(c_q, g_cq)
    q = (c_q @ w_uq).reshape(B, T, ATT_HEADS, ATT_HEAD_DIM)
    q_idx = (c_q @ w_iq).reshape(B, T, IDX_HEADS, IDX_DIM)
    k = k.reshape(B, T, ATT_KV_HEADS, ATT_HEAD_DIM)
    v = v.reshape(B, T, ATT_KV_HEADS, ATT_HEAD_DIM)
    k_idx = layer_norm(k_idx, g_ik, b_ik)
    q = apply_rope(q, cos_q[:, :, None], sin_q[:, :, None])
    k = apply_rope(k, cos_q[:, :, None], sin_q[:, :, None])
    q_idx = apply_rope(q_idx, cos_i[:, :, None], sin_i[:, :, None])
    k_idx = apply_rope(k_idx, cos_i, sin_i)
    w_idx = w_idx * (IDX_HEADS ** -0.5 * IDX_DIM ** -0.5)

    topk = min(INDEX_TOPK, T // 4)
    nb = T // Q_BLOCK
    group = ATT_HEADS // ATT_KV_HEADS
    key_pos = jnp.arange(T)

    def blocks(a):
        return jnp.moveaxis(a.reshape(B, nb, Q_BLOCK, *a.shape[2:]), 1, 0)

    def attend_block(args):
        qb, qib, wb, start = args
        t_pos = start + jnp.arange(Q_BLOCK)
        causal = key_pos[None, :] <= t_pos[:, None]
        rel = jax.nn.relu(jnp.einsum('bqhd,bsd->bhqs', qib, k_idx))
        score = jnp.einsum('bqh,bhqs->bqs', wb, rel).astype(jnp.float32)
        score = jnp.where(causal[None], score, -jnp.inf)
        _, sel = lax.top_k(score, topk)
        valid = sel <= t_pos[None, :, None]
        k_sel = jax.vmap(lambda kk, ii: kk[ii])(k, sel)
        v_sel = jax.vmap(lambda vv, ii: vv[ii])(v, sel)
        qg = qb.reshape(B, Q_BLOCK, ATT_KV_HEADS, group, ATT_HEAD_DIM)
        logits = jnp.einsum('bqgrd,bqkgd->bqgrk', qg, k_sel).astype(jnp.float32) * (ATT_HEAD_DIM ** -0.5)
        logits = jnp.where(valid[:, :, None, None, :], logits, -jnp.inf)
        p = jax.nn.softmax(logits, axis=-1).astype(dt)
        o = jnp.einsum('bqgrk,bqkgd->bqgrd', p, v_sel)
        return o.reshape(B, Q_BLOCK, ATT_HEADS * ATT_HEAD_DIM)

    out = lax.map(attend_block, (blocks(q), blocks(q_idx), blocks(w_idx), jnp.arange(nb) * Q_BLOCK))
    out = jnp.moveaxis(out, 0, 1).reshape(B, T, ATT_HEADS * ATT_HEAD_DIM)
    return out @ w_o


def chunk_gated_linear(q, k, v, logf):
    B, H, T, K = q.shape
    V = v.shape[-1]
    C = HGRN_CHUNK
    nc = T // C
    mask = jnp.tril(jnp.ones((C, C), dtype=bool))

    def to_chunks(a):
        return jnp.moveaxis(a.reshape(B, H, nc, C, a.shape[-1]), 2, 0)

    def step(S, inp):
        qc, kc, vc, gc = inp
        b = jnp.cumsum(gc, axis=-2)
        diff = b[..., :, None, :] - b[..., None, :, :]
        decay = jnp.exp(jnp.where(mask[:, :, None], diff, -jnp.inf))
        attn = jnp.einsum('bhtsk,bhsk->bhts', decay * qc[..., :, None, :], kc)
        o = jnp.einsum('bhts,bhsv->bhtv', attn, vc) + jnp.einsum('bhtk,bhkv->bhtv', qc * jnp.exp(b), S)
        b_last = b[..., -1:, :]
        S = jnp.exp(b_last[..., 0, :])[..., None] * S + jnp.einsum('bhsk,bhsv->bhkv', kc * jnp.exp(b_last - b), vc)
        return S, o

    S0 = jnp.zeros((B, H, K, V), jnp.float32)
    _, o = lax.scan(step, S0, (to_chunks(q), to_chunks(k), to_chunks(v), to_chunks(logf)))
    return jnp.moveaxis(o, 0, 2).reshape(B, H, T, V)


def hgrn2_mixer(x, lb, w_in, g_norm, w_o):
    B, T, _ = x.shape
    dt = x.dtype
    h = x @ w_in
    q, f, i, g = jnp.split(h, [HGRN_F_DIM, 2 * HGRN_F_DIM, 2 * HGRN_F_DIM + D_MODEL], axis=-1)
    q = jax.nn.silu(q.astype(jnp.float32))
    fg = lb + (1.0 - lb) * jax.nn.sigmoid(f.astype(jnp.float32))
    k = 1.0 - fg
    logf = jnp.log(fg)

    def heads(a, d):
        return jnp.transpose(a.reshape(B, T, HGRN_HEADS, d), (0, 2, 1, 3))

    o = chunk_gated_linear(heads(q, HGRN_EXPAND), heads(k, HGRN_EXPAND),
                           heads(i.astype(jnp.float32), HGRN_V_DIM), heads(logf, HGRN_EXPAND))
    o = jnp.transpose(o, (0, 2, 1, 3))
    o = rms_norm(o, g_norm) * jax.nn.silu(g.astype(jnp.float32).reshape(B, T, HGRN_HEADS, HGRN_V_DIM))
    return o.reshape(B, T, D_MODEL).astype(dt) @ w_o


def swiglu(x, w_gate, w_up, w_down):
    return (jax.nn.silu(x @ w_gate) * (x @ w_up)) @ w_down


def setup_inputs(seed: int = 0) -> dict:
    key = jax.random.key(seed)
    ks = jax.random.split(key, 32)
    f32 = jnp.float32
    nA, nB, D = N_ATTN_LAYERS, N_HGRN_LAYERS, D_MODEL

    def nrm(k, shape, fan_in, scale=1.0):
        return jax.random.normal(k, shape, f32) * (fan_in ** -0.5) * scale

    x = jax.random.normal(ks[0], (BATCH, SEQ, D), f32)
    offset = jax.random.randint(ks[1], (BATCH, 1), 0, 1024, dtype=jnp.int32)
    positions = offset + jnp.arange(SEQ, dtype=jnp.int32)[None, :]

    att_w_in = jnp.concatenate([
        nrm(ks[2], (nA, D, Q_LORA_RANK), D),
        nrm(ks[3], (nA, D, ATT_KV_DIM), D),
        nrm(ks[4], (nA, D, ATT_KV_DIM), D, DEEPNORM_BETA),
        nrm(ks[5], (nA, D, IDX_DIM), D),
        nrm(ks[6], (nA, D, IDX_HEADS), D)], axis=-1)
    att_g_cq = 1.0 + 0.02 * jax.random.normal(ks[7], (nA, Q_LORA_RANK), f32)
    att_w_uq = nrm(ks[8], (nA, Q_LORA_RANK, ATT_HEADS * ATT_HEAD_DIM), Q_LORA_RANK)
    att_w_iq = nrm(ks[9], (nA, Q_LORA_RANK, IDX_HEADS * IDX_DIM), Q_LORA_RANK)
    att_g_ik = 1.0 + 0.02 * jax.random.normal(ks[10], (nA, IDX_DIM), f32)
    att_b_ik = 0.02 * jax.random.normal(ks[11], (nA, IDX_DIM), f32)
    att_w_o = nrm(ks[12], (nA, ATT_HEADS * ATT_HEAD_DIM, D), ATT_HEADS * ATT_HEAD_DIM, DEEPNORM_BETA)

    hgrn_lb_logits = 0.5 * jax.random.normal(ks[13], (DEPTH, HGRN_F_DIM), f32)
    hgrn_w_in = jnp.concatenate([
        nrm(ks[14], (nB, D, HGRN_F_DIM), D),
        nrm(ks[15], (nB, D, HGRN_F_DIM), D),
        nrm(ks[16], (nB, D, D), D, DEEPNORM_BETA),
        nrm(ks[17], (nB, D, D), D)], axis=-1)
    hgrn_g_norm = 1.0 + 0.02 * jax.random.normal(ks[18], (nB, HGRN_V_DIM), f32)
    hgrn_w_o = nrm(ks[19], (nB, D, D), D, DEEPNORM_BETA)

    ffn_w_gate = nrm(ks[20], (DEPTH, D, D_FF), D, DEEPNORM_BETA)
    ffn_w_up = nrm(ks[21], (DEPTH, D, D_FF), D, DEEPNORM_BETA)
    ffn_w_down = nrm(ks[22], (DEPTH, D_FF, D), D_FF, DEEPNORM_BETA)

    ln_g = 1.0 + 0.02 * jax.random.normal(ks[23], (DEPTH, 2, D), f32)
    ln_b = 0.02 * jax.random.normal(ks[24], (DEPTH, 2, D), f32)

    return {"x": x, "positions": positions,
            "att_w_in": att_w_in, "att_g_cq": att_g_cq, "att_w_uq": att_w_uq, "att_w_iq": att_w_iq,
            "att_g_ik": att_g_ik, "att_b_ik": att_b_ik, "att_w_o": att_w_o,
            "hgrn_lb_logits": hgrn_lb_logits, "hgrn_w_in": hgrn_w_in, "hgrn_g_norm": hgrn_g_norm,
            "hgrn_w_o": hgrn_w_o,
            "ffn_w_gate": ffn_w_gate, "ffn_w_up": ffn_w_up, "ffn_w_down": ffn_w_down,
            "ln_g": ln_g, "ln_b": ln_b}


def reference(x, positions, att_w_in, att_g_cq, att_w_uq, att_w_iq, att_g_ik, att_b_ik, att_w_o,
              hgrn_lb_logits, hgrn_w_in, hgrn_g_norm, hgrn_w_o,
              ffn_w_gate, ffn_w_up, ffn_w_down, ln_g, ln_b):
    cos_q, sin_q = rope_tables(positions, ATT_ROT_DIM)
    cos_i, sin_i = rope_tables(positions, IDX_ROT_DIM)
    lb_all = jnp.cumsum(jax.nn.softmax(hgrn_lb_logits.astype(jnp.float32), axis=0), axis=0)
    lb_all = lb_all - lb_all[0]
    h = x
    for layer in range(DEPTH):
        j = layer // N_MIXERS
        if layer % N_MIXERS == 0:
            mix = dsa_mixer(h, cos_q, sin_q, cos_i, sin_i, att_w_in[j], att_g_cq[j], att_w_uq[j],
                            att_w_iq[j], att_g_ik[j], att_b_ik[j], att_w_o[j])
        else:
            mix = hgrn2_mixer(h, lb_all[layer], hgrn_w_in[j], hgrn_g_norm[j], hgrn_w_o[j])
        h = layer_norm(DEEPNORM_ALPHA * h + mix, ln_g[layer, 0], ln_b[layer, 0])
        ffn = swiglu(h, ffn_w_gate[layer], ffn_w_up[layer], ffn_w_down[layer])
        h = layer_norm(DEEPNORM_ALPHA * h + ffn, ln_g[layer, 1], ln_b[layer, 1])
    return h
```

```python
import functools

import jax
import jax.numpy as jnp
from jax import lax
from jax.experimental import pallas as pl
from jax.experimental.pallas import tpu as pltpu

F32 = jnp.float32
BF16 = jnp.bfloat16

D_MODEL = 1024
DEPTH = 2
ATT_HEADS = 16
ATT_HEAD_DIM = 64
ATT_KV_HEADS = 2
Q_LORA_RANK = 256
IDX_HEADS = 8
IDX_DIM = 128
INDEX_TOPK = 256
ATT_KV_DIM = ATT_KV_HEADS * ATT_HEAD_DIM
HGRN_EXPAND = 128
HGRN_HEADS = D_MODEL // HGRN_EXPAND
D_FF = 2816
ROPE_THETA = 500000.0
ATT_ROT_DIM = ATT_HEAD_DIM // 4
IDX_ROT_DIM = IDX_DIM // 4
DEEPNORM_ALPHA = (2 * DEPTH) ** 0.25
LN_EPS = 1e-5
RMS_EPS = 1e-6

LANES = 128
VMEM_LIMIT = 56 * 1024 * 1024
ATT_IN_PAD = 768
MASKED = -1e30
NT_DIMS = (((1,), (1,)), ((), ()))
TN_DIMS = (((0,), (0,)), ((), ()))


def _const_spec(shape):
    nd = len(shape)
    return pl.BlockSpec(shape, lambda *_: (0,) * nd, pipeline_mode=pl.Buffered(1))


def _layer_norm(x, g, b):
    mu = jnp.mean(x, axis=-1, keepdims=True)
    xc = x - mu
    var = jnp.mean(xc * xc, axis=-1, keepdims=True)
    return xc * lax.rsqrt(var + LN_EPS) * g + b


def _sigmoid(x):
    return 1.0 / (1.0 + jnp.exp(-x))


def _rope(t, c, s, half, period):
    lane = lax.broadcasted_iota(jnp.int32, t.shape, 1) % period
    up = pltpu.roll(t, LANES - half, 1)
    dn = pltpu.roll(t, half, 1)
    return t * c + jnp.where(lane < half, up, dn) * s


def _dsa_proj_kernel(x_ref, w_in_ref, g_cq_ref, w_uq_ref, w_iq_ref, g_ik_ref, b_ik_ref,
                     cq_ref, sq_ref, ci_ref, si_ref,
                     q_ref, k_ref, v_ref, qi_ref, ki_ref, wi_ref):
    xb = x_ref[...].astype(BF16)
    h = jnp.dot(xb, w_in_ref[...], preferred_element_type=F32)
    c = h[:, :Q_LORA_RANK]
    ms = jnp.mean(c * c, axis=-1, keepdims=True)
    cb = (c * lax.rsqrt(ms + RMS_EPS) * g_cq_ref[...]).astype(BF16)
    q = jnp.dot(cb, w_uq_ref[...], preferred_element_type=F32)
    qi = jnp.dot(cb, w_iq_ref[...], preferred_element_type=F32)
    cq, sq, ci, si = cq_ref[...], sq_ref[...], ci_ref[...], si_ref[...]
    scale = ATT_HEAD_DIM ** -0.5
    for j in range(D_MODEL // LANES):
        sl = slice(j * LANES, (j + 1) * LANES)
        q_ref[:, sl] = (_rope(q[:, sl], cq, sq, ATT_ROT_DIM // 2, ATT_HEAD_DIM) * scale).astype(BF16)
        qi_ref[:, sl] = _rope(qi[:, sl], ci, si, IDX_ROT_DIM // 2, IDX_DIM).astype(BF16)
    o = Q_LORA_RANK
    k_ref[...] = _rope(h[:, o:o + LANES], cq, sq, ATT_ROT_DIM // 2, ATT_HEAD_DIM).astype(BF16)
    v_ref[...] = h[:, o + LANES:o + 2 * LANES].astype(BF16)
    ki = _layer_norm(h[:, o + 2 * LANES:o + 3 * LANES], g_ik_ref[...], b_ik_ref[...])
    ki_ref[...] = _rope(ki, ci, si, IDX_ROT_DIM // 2, IDX_DIM).astype(BF16)
    wi_ref[...] = h[:, o + 3 * LANES:o + 4 * LANES] * (IDX_HEADS ** -0.5 * IDX_DIM ** -0.5)


def _dsa_proj(x2, w_in_p, g_cq, w_uq, w_iq, g_ik, b_ik, cq, sq, ci, si, tm):
    n = x2.shape[0]
    row = lambda w: pl.BlockSpec((tm, w), lambda i: (i, 0))
    return pl.pallas_call(
        _dsa_proj_kernel,
        grid=(n // tm,),
        in_specs=[row(D_MODEL), _const_spec(w_in_p.shape), _const_spec(g_cq.shape),
                  _const_spec(w_uq.shape), _const_spec(w_iq.shape),
                  _const_spec(g_ik.shape), _const_spec(b_ik.shape),
                  row(LANES), row(LANES), row(LANES), row(LANES)],
        out_specs=[row(D_MODEL), row(LANES), row(LANES), row(D_MODEL), row(LANES), row(LANES)],
        out_shape=[jax.ShapeDtypeStruct((n, D_MODEL), BF16), jax.ShapeDtypeStruct((n, LANES), BF16),
                   jax.ShapeDtypeStruct((n, LANES), BF16), jax.ShapeDtypeStruct((n, D_MODEL), BF16),
                   jax.ShapeDtypeStruct((n, LANES), BF16), jax.ShapeDtypeStruct((n, LANES), F32)],
        compiler_params=pltpu.CompilerParams(dimension_semantics=("arbitrary",),
                                             vmem_limit_bytes=VMEM_LIMIT),
        name="dsa_proj",
    )(x2, w_in_p, g_cq, w_uq, w_iq, g_ik, b_ik, cq, sq, ci, si)


INT_MIN = -2 ** 31
KEY_NEG_INF = INT_MIN + 0x7FFFFF


def _key_to_float(ukey):
    key = jnp.maximum(ukey ^ jnp.int32(INT_MIN), jnp.int32(KEY_NEG_INF))
    bits = key ^ ((key >> 31) & jnp.int32(0x7FFFFFFF))
    return lax.bitcast_convert_type(bits, F32)


def _dsa_attn_kernel(q_ref, qi_ref, wi_ref, k_ref, v_ref, ki_ref, o_ref, s_ref, *, qt, rb, topk, seq):
    i = pl.program_id(1)
    nch = i + 1
    kc = qt
    ntile = kc // LANES

    wi = wi_ref[...]

    def score_chunk(c, carry):
        kblk = ki_ref[pl.ds(pl.multiple_of(c * kc, kc), kc), :]
        acc = jnp.zeros((qt, kc), F32)
        for h in range(IDX_HEADS):
            r = lax.dot_general(qi_ref[:, h * IDX_DIM:(h + 1) * IDX_DIM], kblk, NT_DIMS,
                                preferred_element_type=F32)
            acc = acc + jnp.maximum(r, 0.0) * wi[:, h:h + 1]
        s_ref[c] = acc
        return carry

    lax.fori_loop(0, nch, score_chunk, 0)
    rows = lax.broadcasted_iota(jnp.int32, (qt, kc), 0)
    cols = lax.broadcasted_iota(jnp.int32, (qt, kc), 1)
    s_ref[i] = jnp.where(cols <= rows, s_ref[i], -jnp.inf)

    def select_rows(r, carry):
        r0 = pl.multiple_of(r * rb, rb)

        def count(pred):
            def body(c, acc):
                m = pred(s_ref[c, pl.ds(r0, rb), :], c)
                for j in range(ntile):
                    acc = acc + jnp.where(m[:, j * LANES:(j + 1) * LANES], 1.0, 0.0)
                return acc
            acc = lax.fori_loop(0, nch, body, jnp.zeros((rb, LANES), F32))
            return jnp.sum(acc, axis=1, keepdims=True)

        def value_bit(it, ans):
            cand = ans | lax.shift_left(jnp.int32(1), 31 - it)
            cf = _key_to_float(cand)
            cnt = count(lambda blk, c: blk >= cf)
            return jnp.where(cnt >= topk, cand, ans)

        tau = _key_to_float(lax.fori_loop(0, 32, value_bit, jnp.zeros((rb, 1), jnp.int32)))
        cnt_gt = count(lambda blk, c: blk > tau)
        cnt_ge = count(lambda blk, c: blk >= tau)
        need = topk - cnt_gt
        finite = tau > -jnp.inf
        excess = jnp.where((cnt_ge - cnt_gt > need) & finite, 1.0, 0.0)
        lcols = lax.broadcasted_iota(jnp.int32, (rb, kc), 1)

        def index_cut():
            def index_bit(it, p):
                cand = p | lax.shift_left(jnp.int32(1), (seq - 1).bit_length() - 1 - it)
                cnt = count(lambda blk, c: (blk == tau) & (lcols + c * kc < cand))
                return jnp.where(cnt < need, cand, p)
            return lax.fori_loop(0, (seq - 1).bit_length(), index_bit, jnp.zeros((rb, 1), jnp.int32))

        cut = lax.cond(jnp.max(excess) > 0.0, index_cut, lambda: jnp.full((rb, 1), seq, jnp.int32))
        cut = jnp.where(finite, cut, -1)

        def write_bias(c, carry2):
            blk = s_ref[c, pl.ds(r0, rb), :]
            sel = (blk > tau) | ((blk == tau) & (lcols + c * kc <= cut))
            s_ref[c, pl.ds(r0, rb), :] = jnp.where(sel, 0.0, MASKED)
            return carry2

        lax.fori_loop(0, nch, write_bias, 0)
        return carry

    lax.fori_loop(0, qt // rb, select_rows, 0)

    group = ATT_HEADS // ATT_KV_HEADS
    for h in range(ATT_HEADS):
        g = h // group
        qh = q_ref[:, h * ATT_HEAD_DIM:(h + 1) * ATT_HEAD_DIM]
        gs = slice(g * ATT_HEAD_DIM, (g + 1) * ATT_HEAD_DIM)

        def attend(c, carry, qh=qh, gs=gs):
            m, l, acc = carry
            c0 = pl.multiple_of(c * kc, kc)
            s = lax.dot_general(qh, k_ref[pl.ds(c0, kc), gs], NT_DIMS,
                                preferred_element_type=F32) + s_ref[c]
            m_new = jnp.maximum(m, jnp.max(s, axis=1, keepdims=True))
            alpha = jnp.exp(m - m_new)
            p = jnp.exp(s - m_new)
            l = alpha * l + jnp.sum(p, axis=1, keepdims=True)
            acc = alpha * acc + jnp.dot(p.astype(BF16), v_ref[pl.ds(c0, kc), gs],
                                        preferred_element_type=F32)
            return m_new, l, acc

        init = (jnp.full((qt, 1), MASKED, F32), jnp.zeros((qt, 1), F32),
                jnp.zeros((qt, ATT_HEAD_DIM), F32))
        _, l, acc = lax.fori_loop(0, nch, attend, init)
        o_ref[:, h * ATT_HEAD_DIM:(h + 1) * ATT_HEAD_DIM] = (acc / l).astype(BF16)


def _dsa_attn(q, qi, wi, k, v, ki, batch, seq, qt, rb):
    nq = seq // qt
    topk = min(INDEX_TOPK, seq // 4)
    qrow = lambda w: pl.BlockSpec((qt, w), lambda b, i: (b * nq + i, 0))
    kv = pl.BlockSpec((seq, LANES), lambda b, i: (b, 0))
    return pl.pallas_call(
        functools.partial(_dsa_attn_kernel, qt=qt, rb=rb, topk=topk, seq=seq),
        grid=(batch, nq),
        in_specs=[qrow(D_MODEL), qrow(D_MODEL), qrow(LANES), kv, kv, kv],
        out_specs=qrow(D_MODEL),
        out_shape=jax.ShapeDtypeStruct((batch * seq, D_MODEL), BF16),
        scratch_shapes=[pltpu.VMEM((nq, qt, qt), F32)],
        compiler_params=pltpu.CompilerParams(dimension_semantics=("arbitrary", "arbitrary"),
                                             vmem_limit_bytes=VMEM_LIMIT),
        name="dsa_attn",
    )(q, qi, wi, k, v, ki)


FF_CHUNKS = ((0, 768), (768, 768), (1536, 768), (2304, 512))


def _post_kernel(a_ref, x_ref, wo_ref, g1_ref, b1_ref, wg_ref, wu_ref, wd_ref, g2_ref, b2_ref, o_ref):
    y = jnp.dot(a_ref[...], wo_ref[...], preferred_element_type=F32)
    h = _layer_norm(DEEPNORM_ALPHA * x_ref[...] + y, g1_ref[...], b1_ref[...])
    hb = h.astype(BF16)
    f = jnp.zeros(h.shape, F32)
    for start, size in FF_CHUNKS:
        gate = jnp.dot(hb, wg_ref[:, start:start + size], preferred_element_type=F32)
        up = jnp.dot(hb, wu_ref[:, start:start + size], preferred_element_type=F32)
        act = (gate * _sigmoid(gate) * up).astype(BF16)
        f = f + jnp.dot(act, wd_ref[start:start + size, :], preferred_element_type=F32)
    o_ref[...] = _layer_norm(DEEPNORM_ALPHA * h + f, g2_ref[...], b2_ref[...])


def _post(a, x2, wo, g1, b1, wg, wu, wd, g2, b2, tm):
    n = x2.shape[0]
    row = pl.BlockSpec((tm, D_MODEL), lambda i: (i, 0))
    return pl.pallas_call(
        _post_kernel,
        grid=(n // tm,),
        in_specs=[row, row] + [_const_spec(t.shape) for t in (wo, g1, b1, wg, wu, wd, g2, b2)],
        out_specs=row,
        out_shape=jax.ShapeDtypeStruct((n, D_MODEL), F32),
        compiler_params=pltpu.CompilerParams(dimension_semantics=("arbitrary",),
                                             vmem_limit_bytes=VMEM_LIMIT),
        name="post_ffn",
    )(a, x2, wo, g1, b1, wg, wu, wd, g2, b2)


CHUNK = 64
SUB = 16
SUBLANES = 8


def _split3(x):
    hi = x.astype(BF16)
    r1 = x - hi.astype(F32)
    mid = r1.astype(BF16)
    lo = (r1 - mid.astype(F32)).astype(BF16)
    return hi, mid, lo


def _hgrn_chunk(q, k, v, lf, st, b_s, k_s, v_s):
    rowi = lax.broadcasted_iota(jnp.int32, (CHUNK, CHUNK), 0)
    coli = lax.broadcasted_iota(jnp.int32, (CHUNK, CHUNK), 1)
    tri = jnp.where(rowi >= coli, 1.0, 0.0).astype(BF16)
    hi, mid, lo = _split3(lf)
    b = (jnp.dot(tri, hi, preferred_element_type=F32) + jnp.dot(tri, mid, preferred_element_type=F32)
         + jnp.dot(tri, lo, preferred_element_type=F32))
    b_s[...] = b
    k_s[...] = k
    v_s[...] = v
    b_last = b_s[CHUNK - 1:CHUNK, :]

    qh = (q * jnp.exp(b)).astype(BF16)
    o = lax.dot_general(qh, st.astype(BF16), NT_DIMS, preferred_element_type=F32)

    outs = []
    nsub = CHUNK // SUB
    for j in range(nsub):
        r0 = j * SUB
        oj = o[r0:r0 + SUB]
        for half in range(SUB // SUBLANES):
            t0 = r0 + half * SUBLANES
            bq = b[t0:r0 + SUB]
            qq = q[t0:r0 + SUB]
            trow = lax.broadcasted_iota(jnp.int32, bq.shape, 0)
            acc = jnp.zeros(bq.shape, F32)
            for sl in range(SUBLANES):
                s = t0 + sl
                d = jnp.where(trow >= sl, bq - b_s[s:s + 1, :], -jnp.inf)
                p = jnp.exp(d) * (qq * k_s[s:s + 1, :])
                acc = acc + jnp.sum(p, axis=1, keepdims=True) * v_s[s:s + 1, :]
            if half == 0:
                oj = oj + acc
            else:
                oj = jnp.concatenate([oj[:half * SUBLANES], oj[half * SUBLANES:] + acc], axis=0)
        outs.append(oj)
    o = jnp.concatenate(outs, axis=0)

    for j in range(nsub - 1):
        r0, r1 = j * SUB, (j + 1) * SUB
        e_j = b_s[r1 - 1:r1, :]
        kt = (k[r0:r1] * jnp.exp(e_j - b[r0:r1])).astype(BF16)
        qt = (q[r1:] * jnp.exp(b[r1:] - e_j)).astype(BF16)
        a = lax.dot_general(qt, kt, NT_DIMS, preferred_element_type=F32)
        upd = jnp.dot(a.astype(BF16), v[r0:r1].astype(BF16), preferred_element_type=F32)
        o = jnp.concatenate([o[:r1], o[r1:] + upd], axis=0)

    kh = (k * jnp.exp(b_last - b)).astype(BF16)
    st = st * jnp.exp(b_last) + lax.dot_general(v.astype(BF16), kh, TN_DIMS, preferred_element_type=F32)
    return o, st


def _hgrn_kernel(x_ref, w_ref, lb_ref, gn_ref, a_ref, st_ref, q_s, k_s, v_s, lf_s, o_s, bc_s, kc_s, vc_s, *, tc):
    @pl.when(pl.program_id(1) == 0)
    def _():
        st_ref[...] = jnp.zeros(st_ref.shape, F32)

    d = D_MODEL
    xb = x_ref[...].astype(BF16)
    lb = lb_ref[...]
    hq = jnp.dot(xb, w_ref[:, 0:d], preferred_element_type=F32)
    q = hq * _sigmoid(hq)
    hf = jnp.dot(xb, w_ref[:, d:2 * d], preferred_element_type=F32)
    fg = lb + (1.0 - lb) * _sigmoid(hf)
    k = 1.0 - fg
    lf = jnp.log(fg)
    v = jnp.dot(xb, w_ref[:, 2 * d:3 * d], preferred_element_type=F32)
    for h in range(HGRN_HEADS):
        sl = slice(h * HGRN_EXPAND, (h + 1) * HGRN_EXPAND)
        q_s[h] = q[:, sl]
        k_s[h] = k[:, sl]
        v_s[h] = v[:, sl]
        lf_s[h] = lf[:, sl]

    def head_body(h, carry):
        def chunk_body(ci, st):
            rows = pl.ds(pl.multiple_of(ci * CHUNK, CHUNK), CHUNK)
            o, st = _hgrn_chunk(q_s[h, rows, :], k_s[h, rows, :], v_s[h, rows, :], lf_s[h, rows, :],
                                st, bc_s, kc_s, vc_s)
            o_s[h, rows, :] = o
            return st
        st_ref[h] = lax.fori_loop(0, tc // CHUNK, chunk_body, st_ref[h])
        return carry

    lax.fori_loop(0, HGRN_HEADS, head_body, 0)

    hg = jnp.dot(xb, w_ref[:, 3 * d:4 * d], preferred_element_type=F32)
    gate = hg * _sigmoid(hg)
    gn = gn_ref[...]
    for h in range(HGRN_HEADS):
        sl = slice(h * HGRN_EXPAND, (h + 1) * HGRN_EXPAND)
        o = o_s[h]
        ms = jnp.mean(o * o, axis=-1, keepdims=True)
        a_ref[:, sl] = (o * lax.rsqrt(ms + RMS_EPS) * gn * gate[:, sl]).astype(BF16)


def _hgrn(x2, w_in, lb, g_norm, batch, seq, tc):
    nt = seq // tc
    row = pl.BlockSpec((tc, D_MODEL), lambda b, t: (b * nt + t, 0))
    hm = lambda: pltpu.VMEM((HGRN_HEADS, tc, HGRN_EXPAND), F32)
    cm = lambda: pltpu.VMEM((CHUNK, HGRN_EXPAND), F32)
    return pl.pallas_call(
        functools.partial(_hgrn_kernel, tc=tc),
        grid=(batch, nt),
        in_specs=[row, _const_spec(w_in.shape), _const_spec(lb.shape), _const_spec(g_norm.shape)],
        out_specs=row,
        out_shape=jax.ShapeDtypeStruct((batch * seq, D_MODEL), BF16),
        scratch_shapes=[pltpu.VMEM((HGRN_HEADS, HGRN_EXPAND, HGRN_EXPAND), F32),
                        hm(), hm(), hm(), hm(), hm(), cm(), cm(), cm()],
        compiler_params=pltpu.CompilerParams(dimension_semantics=("arbitrary", "arbitrary"),
                                             vmem_limit_bytes=VMEM_LIMIT),
        name="hgrn2",
    )(x2, w_in, lb, g_norm)


def _rope_tables(positions, rot_dim, period):
    half = rot_dim // 2
    inv = ROPE_THETA ** (-jnp.arange(0, rot_dim, 2, dtype=F32) / rot_dim)
    ang = positions.astype(F32)[..., None] * inv
    cos, sin = jnp.cos(ang), jnp.sin(ang)
    rest = ang.shape[:-1] + (period - 2 * half,)
    c = jnp.concatenate([cos, cos, jnp.ones(rest, F32)], axis=-1)
    s = jnp.concatenate([-sin, sin, jnp.zeros(rest, F32)], axis=-1)
    reps = LANES // period
    c = jnp.tile(c, (1, 1, reps)).reshape(-1, LANES)
    s = jnp.tile(s, (1, 1, reps)).reshape(-1, LANES)
    return c, s


def _row(v):
    return v.reshape(1, -1).astype(F32)


def _dsa_layer(x2, positions, w_in, g_cq, w_uq, w_iq, g_ik, b_ik, batch, seq, tm, qt, rb):
    cq, sq = _rope_tables(positions, ATT_ROT_DIM, ATT_HEAD_DIM)
    ci, si = _rope_tables(positions, IDX_ROT_DIM, IDX_DIM)
    w_in_p = jnp.pad(w_in, ((0, 0), (0, ATT_IN_PAD - w_in.shape[1]))).astype(BF16)
    q, k, v, qi, ki, wi = _dsa_proj(x2, w_in_p, _row(g_cq), w_uq.astype(BF16), w_iq.astype(BF16),
                                    _row(g_ik), _row(b_ik), cq, sq, ci, si, tm)
    return _dsa_attn(q, qi, wi, k, v, ki, batch, seq, qt, rb)


def kernel(x, positions, att_w_in, att_g_cq, att_w_uq, att_w_iq, att_g_ik, att_b_ik, att_w_o,
           hgrn_lb_logits, hgrn_w_in, hgrn_g_norm, hgrn_w_o,
           ffn_w_gate, ffn_w_up, ffn_w_down, ln_g, ln_b):
    batch, seq, d = x.shape
    n = batch * seq
    tm = min(512, n)
    qt = min(256, seq)
    tc = min(256, seq)
    lb_all = jnp.cumsum(jax.nn.softmax(hgrn_lb_logits.astype(F32), axis=0), axis=0)
    lb_all = lb_all - lb_all[0]

    h = x.reshape(n, d)
    for layer in range(DEPTH):
        j = layer // 2
        if layer % 2 == 0:
            a = _dsa_layer(h, positions, att_w_in[j], att_g_cq[j], att_w_uq[j], att_w_iq[j],
                           att_g_ik[j], att_b_ik[j], batch, seq, tm, qt, 64)
            w_o = att_w_o[j]
        else:
            a = _hgrn(h, hgrn_w_in[j].astype(BF16), _row(lb_all[layer]), _row(hgrn_g_norm[j]),
                      batch, seq, tc)
            w_o = hgrn_w_o[j]
        h = _post(a, h, w_o.astype(BF16), _row(ln_g[layer, 0]), _row(ln_b[layer, 0]),
                  ffn_w_gate[layer].astype(BF16), ffn_w_up[layer].astype(BF16),
                  ffn_w_down[layer].astype(BF16), _row(ln_g[layer, 1]), _row(ln_b[layer, 1]), tm)
    return h.reshape(batch, seq, d)
```

```python
import functools

import jax
import jax.numpy as jnp
from jax import lax
from jax.experimental import pallas as pl
from jax.experimental.pallas import tpu as pltpu

F32 = jnp.float32
BF16 = jnp.bfloat16

D_MODEL = 1024
DEPTH = 2
ATT_HEADS = 16
ATT_HEAD_DIM = 64
ATT_KV_HEADS = 2
Q_LORA_RANK = 256
IDX_HEADS = 8
IDX_DIM = 128
INDEX_TOPK = 256
ATT_KV_DIM = ATT_KV_HEADS * ATT_HEAD_DIM
HGRN_EXPAND = 128
HGRN_HEADS = D_MODEL // HGRN_EXPAND
D_FF = 2816
ROPE_THETA = 500000.0
ATT_ROT_DIM = ATT_HEAD_DIM // 4
IDX_ROT_DIM = IDX_DIM // 4
DEEPNORM_ALPHA = (2 * DEPTH) ** 0.25
LN_EPS = 1e-5
RMS_EPS = 1e-6

LANES = 128
SUBLANES = 8
VMEM_LIMIT = 56 * 1024 * 1024
ATT_IN_PAD = 768
MASKED = -1e30
NT_DIMS = (((1,), (1,)), ((), ()))
TN_DIMS = (((0,), (0,)), ((), ()))


def _const_spec(shape):
    nd = len(shape)
    return pl.BlockSpec(shape, lambda *_: (0,) * nd, pipeline_mode=pl.Buffered(1))


def _layer_norm(x, g, b):
    mu = jnp.mean(x, axis=-1, keepdims=True)
    xc = x - mu
    var = jnp.mean(xc * xc, axis=-1, keepdims=True)
    return xc * lax.rsqrt(var + LN_EPS) * g + b


def _sigmoid(x):
    return 1.0 / (1.0 + jnp.exp(-x))


def _rope(t, c, s, half, period):
    lane = lax.broadcasted_iota(jnp.int32, t.shape, 1) % period
    up = pltpu.roll(t, LANES - half, 1)
    dn = pltpu.roll(t, half, 1)
    return t * c + jnp.where(lane < half, up, dn) * s


def _dsa_proj_kernel(x_ref, w_in_ref, g_cq_ref, w_uq_ref, w_iq_ref, g_ik_ref, b_ik_ref,
                     cq_ref, sq_ref, ci_ref, si_ref,
                     q_ref, k_ref, vt_ref, qi_ref, ki_ref, wit_ref):
    xb = x_ref[...].astype(BF16)
    h = jnp.dot(xb, w_in_ref[...], preferred_element_type=F32)
    c = h[:, :Q_LORA_RANK]
    ms = jnp.mean(c * c, axis=-1, keepdims=True)
    cb = (c * lax.rsqrt(ms + RMS_EPS) * g_cq_ref[...]).astype(BF16)
    q = jnp.dot(cb, w_uq_ref[...], preferred_element_type=F32)
    qi = jnp.dot(cb, w_iq_ref[...], preferred_element_type=F32)
    cq, sq, ci, si = cq_ref[...], sq_ref[...], ci_ref[...], si_ref[...]
    scale = ATT_HEAD_DIM ** -0.5
    for j in range(D_MODEL // LANES):
        sl = slice(j * LANES, (j + 1) * LANES)
        q_ref[:, sl] = (_rope(q[:, sl], cq, sq, ATT_ROT_DIM // 2, ATT_HEAD_DIM) * scale).astype(BF16)
        qi_ref[:, sl] = _rope(qi[:, sl], ci, si, IDX_ROT_DIM // 2, IDX_DIM).astype(BF16)
    o = Q_LORA_RANK
    k_ref[...] = _rope(h[:, o:o + LANES], cq, sq, ATT_ROT_DIM // 2, ATT_HEAD_DIM).astype(BF16)
    vt = h[:, o + LANES:o + 2 * LANES].T
    tm = vt.shape[1]
    ones_row = jnp.where(lax.broadcasted_iota(jnp.int32, (ATT_HEAD_DIM, tm), 0) == 0, 1.0, 0.0)
    for g in range(ATT_KV_HEADS):
        vt_ref[g * LANES:g * LANES + ATT_HEAD_DIM, :] = vt[g * ATT_HEAD_DIM:(g + 1) * ATT_HEAD_DIM].astype(BF16)
        vt_ref[g * LANES + ATT_HEAD_DIM:(g + 1) * LANES, :] = ones_row.astype(BF16)
    ki = _layer_norm(h[:, o + 2 * LANES:o + 3 * LANES], g_ik_ref[...], b_ik_ref[...])
    ki_ref[...] = _rope(ki, ci, si, IDX_ROT_DIM // 2, IDX_DIM).astype(BF16)
    wi = h[:, o + 3 * LANES:o + 4 * LANES] * (IDX_HEADS ** -0.5 * IDX_DIM ** -0.5)
    wit_ref[...] = wi.T[:IDX_HEADS]


def _dsa_proj(x2, w_in_p, g_cq, w_uq, w_iq, g_ik, b_ik, cq, sq, ci, si, tm):
    n = x2.shape[0]
    row = lambda w: pl.BlockSpec((tm, w), lambda i: (i, 0))
    col = lambda r: pl.BlockSpec((r, tm), lambda i: (0, i))
    return pl.pallas_call(
        _dsa_proj_kernel,
        grid=(n // tm,),
        in_specs=[row(D_MODEL), _const_spec(w_in_p.shape), _const_spec(g_cq.shape),
                  _const_spec(w_uq.shape), _const_spec(w_iq.shape),
                  _const_spec(g_ik.shape), _const_spec(b_ik.shape),
                  row(LANES), row(LANES), row(LANES), row(LANES)],
        out_specs=[row(D_MODEL), row(LANES), col(ATT_KV_HEADS * LANES), row(D_MODEL), row(LANES),
                   col(IDX_HEADS)],
        out_shape=[jax.ShapeDtypeStruct((n, D_MODEL), BF16), jax.ShapeDtypeStruct((n, LANES), BF16),
                   jax.ShapeDtypeStruct((ATT_KV_HEADS * LANES, n), BF16),
                   jax.ShapeDtypeStruct((n, D_MODEL), BF16),
                   jax.ShapeDtypeStruct((n, LANES), BF16), jax.ShapeDtypeStruct((IDX_HEADS, n), F32)],
        compiler_params=pltpu.CompilerParams(dimension_semantics=("arbitrary",),
                                             vmem_limit_bytes=VMEM_LIMIT),
        name="dsa_proj",
    )(x2, w_in_p, g_cq, w_uq, w_iq, g_ik, b_ik, cq, sq, ci, si)


INT_MIN = -2 ** 31
KEY_NEG_INF = INT_MIN + 0x7FFFFF


def _key_to_float(ukey):
    key = jnp.maximum(ukey ^ jnp.int32(INT_MIN), jnp.int32(KEY_NEG_INF))
    bits = key ^ ((key >> 31) & jnp.int32(0x7FFFFFFF))
    return lax.bitcast_convert_type(bits, F32)


def _dsa_attn_kernel(q_ref, qi_ref, wit_ref, k_ref, vt_ref, ki_ref, o_ref, s_ref, m_ref, acc_ref, ot_ref,
                     *, qt, topk, seq):
    i = pl.program_id(1)
    nch = i + 1
    kc = qt
    krow = lax.broadcasted_iota(jnp.int32, (kc, qt), 0)
    qcol = lax.broadcasted_iota(jnp.int32, (kc, qt), 1)

    def chunk_rows(c):
        return pl.ds(pl.multiple_of(c * kc, kc), kc)

    wit = wit_ref[...]

    def score_chunk(c, carry):
        kblk = ki_ref[chunk_rows(c), :]
        acc = None
        for h in range(IDX_HEADS):
            r = lax.dot_general(kblk, qi_ref[:, h * IDX_DIM:(h + 1) * IDX_DIM], NT_DIMS,
                                preferred_element_type=F32)
            contrib = jnp.maximum(r, 0.0) * wit[h:h + 1, :]
            acc = contrib if acc is None else acc + contrib
        s_ref[c] = acc
        return carry

    lax.fori_loop(0, nch, score_chunk, 0)
    s_ref[i] = jnp.where(krow <= qcol, s_ref[i], -jnp.inf)

    def count(pred):
        def body(c, acc):
            m = pred(s_ref[c], c)
            for r in range(kc // SUBLANES):
                acc = acc + jnp.where(m[r * SUBLANES:(r + 1) * SUBLANES], 1.0, 0.0)
            return acc
        acc = lax.fori_loop(0, nch, body, jnp.zeros((SUBLANES, qt), F32))
        return jnp.sum(acc, axis=0, keepdims=True)

    def value_bit(it, ans):
        cand = ans | lax.shift_left(jnp.int32(1), 31 - it)
        cf = _key_to_float(cand)
        cnt = count(lambda blk, c: blk >= cf)
        return jnp.where(cnt >= topk, cand, ans)

    tau = _key_to_float(lax.fori_loop(0, 32, value_bit, jnp.zeros((1, qt), jnp.int32)))
    cnt_gt = count(lambda blk, c: blk > tau)
    cnt_ge = count(lambda blk, c: blk >= tau)
    need = topk - cnt_gt
    finite = tau > -jnp.inf
    excess = jnp.where((cnt_ge - cnt_gt > need) & finite, 1.0, 0.0)

    def index_cut():
        nbits = (seq - 1).bit_length()

        def index_bit(it, p):
            cand = p | lax.shift_left(jnp.int32(1), nbits - 1 - it)
            cnt = count(lambda blk, c: (blk == tau) & (krow + c * kc < cand))
            return jnp.where(cnt < need, cand, p)
        return lax.fori_loop(0, nbits, index_bit, jnp.zeros((1, qt), jnp.int32))

    cut = lax.cond(jnp.max(excess) > 0.0, index_cut, lambda: jnp.full((1, qt), seq, jnp.int32))
    cut = jnp.where(finite, cut, -1)

    def write_bias(c, carry):
        blk = s_ref[c]
        sel = (blk > tau) | ((blk == tau) & (krow + c * kc <= cut))
        s_ref[c] = jnp.where(sel, 0.0, MASKED)
        return carry

    lax.fori_loop(0, nch, write_bias, 0)

    group = ATT_HEADS // ATT_KV_HEADS
    m_ref[...] = jnp.full(m_ref.shape, MASKED, F32)
    acc_ref[...] = jnp.zeros(acc_ref.shape, F32)
    for g in range(ATT_KV_HEADS):
        def attend(c, carry, g=g):
            kg = k_ref[chunk_rows(c), g * ATT_HEAD_DIM:(g + 1) * ATT_HEAD_DIM]
            vtg = vt_ref[g * LANES:(g + 1) * LANES, chunk_rows(c)]
            bias = s_ref[c]
            heads = range(g * group, (g + 1) * group)
            sts = [lax.dot_general(kg, q_ref[:, h * ATT_HEAD_DIM:(h + 1) * ATT_HEAD_DIM], NT_DIMS,
                                   preferred_element_type=F32) + bias for h in heads]
            ps, alphas = [], []
            for h, st in zip(heads, sts):
                mx = st[:SUBLANES]
                for r in range(1, kc // SUBLANES):
                    mx = jnp.maximum(mx, st[r * SUBLANES:(r + 1) * SUBLANES])
                m_old = m_ref[h]
                m_new = jnp.maximum(m_old, jnp.max(mx, axis=0, keepdims=True))
                m_ref[h] = m_new
                ps.append(jnp.exp(st - m_new).astype(BF16))
                alphas.append(jnp.exp(m_old - m_new))
            for h, p, alpha in zip(heads, ps, alphas):
                acc_ref[h] = alpha * acc_ref[h] + jnp.dot(vtg, p, preferred_element_type=F32)
            return carry

        lax.fori_loop(0, nch, attend, 0)

    for h in range(ATT_HEADS):
        acc = acc_ref[h]
        ot_ref[h * ATT_HEAD_DIM:(h + 1) * ATT_HEAD_DIM, :] = (
            acc[:ATT_HEAD_DIM] * (1.0 / acc[ATT_HEAD_DIM:ATT_HEAD_DIM + 1]))
    o_ref[...] = ot_ref[...].T.astype(BF16)


def _dsa_attn(q, qi, wit, k, vt, ki, batch, seq, qt):
    nq = seq // qt
    topk = min(INDEX_TOPK, seq // 4)
    qrow = lambda w: pl.BlockSpec((qt, w), lambda b, i: (b * nq + i, 0))
    kv = pl.BlockSpec((seq, LANES), lambda b, i: (b, 0))
    return pl.pallas_call(
        functools.partial(_dsa_attn_kernel, qt=qt, topk=topk, seq=seq),
        grid=(batch, nq),
        in_specs=[qrow(D_MODEL), qrow(D_MODEL),
                  pl.BlockSpec((IDX_HEADS, qt), lambda b, i: (0, b * nq + i)),
                  kv, pl.BlockSpec((ATT_KV_HEADS * LANES, seq), lambda b, i: (0, b)), kv],
        out_specs=qrow(D_MODEL),
        out_shape=jax.ShapeDtypeStruct((batch * seq, D_MODEL), BF16),
        scratch_shapes=[pltpu.VMEM((nq, qt, qt), F32),
                        pltpu.VMEM((ATT_HEADS, 1, qt), F32),
                        pltpu.VMEM((ATT_HEADS, LANES, qt), F32),
                        pltpu.VMEM((D_MODEL, qt), F32)],
        compiler_params=pltpu.CompilerParams(dimension_semantics=("arbitrary", "arbitrary"),
                                             vmem_limit_bytes=VMEM_LIMIT),
        name="dsa_attn",
    )(q, qi, wit, k, vt, ki)


FF_CHUNKS = ((0, 768), (768, 768), (1536, 768), (2304, 512))


def _post_kernel(a_ref, x_ref, wo_ref, g1_ref, b1_ref, wg_ref, wu_ref, wd_ref, g2_ref, b2_ref, o_ref):
    y = jnp.dot(a_ref[...], wo_ref[...], preferred_element_type=F32)
    h = _layer_norm(DEEPNORM_ALPHA * x_ref[...] + y, g1_ref[...], b1_ref[...])
    hb = h.astype(BF16)
    f = jnp.zeros(h.shape, F32)
    for start, size in FF_CHUNKS:
        gate = jnp.dot(hb, wg_ref[:, start:start + size], preferred_element_type=F32)
        up = jnp.dot(hb, wu_ref[:, start:start + size], preferred_element_type=F32)
        act = (gate * _sigmoid(gate) * up).astype(BF16)
        f = f + jnp.dot(act, wd_ref[start:start + size, :], preferred_element_type=F32)
    o_ref[...] = _layer_norm(DEEPNORM_ALPHA * h + f, g2_ref[...], b2_ref[...])


def _post(a, x2, wo, g1, b1, wg, wu, wd, g2, b2, tm):
    n = x2.shape[0]
    row = pl.BlockSpec((tm, D_MODEL), lambda i: (i, 0))
    return pl.pallas_call(
        _post_kernel,
        grid=(n // tm,),
        in_specs=[row, row] + [_const_spec(t.shape) for t in (wo, g1, b1, wg, wu, wd, g2, b2)],
        out_specs=row,
        out_shape=jax.ShapeDtypeStruct((n, D_MODEL), F32),
        compiler_params=pltpu.CompilerParams(dimension_semantics=("arbitrary",),
                                             vmem_limit_bytes=VMEM_LIMIT),
        name="post_ffn",
    )(a, x2, wo, g1, b1, wg, wu, wd, g2, b2)


CHUNK = 64
SUB = 16


def _split3(x):
    hi = x.astype(BF16)
    r1 = x - hi.astype(F32)
    mid = r1.astype(BF16)
    lo = (r1 - mid.astype(F32)).astype(BF16)
    return hi, mid, lo


def _hgrn_chunk(qs, ks, vs, lfs, sts, b_s, k_s, v_s):
    heads = range(len(qs))
    rowi = lax.broadcasted_iota(jnp.int32, (CHUNK, CHUNK), 0)
    coli = lax.broadcasted_iota(jnp.int32, (CHUNK, CHUNK), 1)
    tri = jnp.where(rowi >= coli, 1.0, 0.0).astype(BF16)
    nsub = CHUNK // SUB

    bs = []
    for h in heads:
        hi, mid, lo = _split3(lfs[h])
        bs.append(jnp.dot(tri, hi, preferred_element_type=F32) + jnp.dot(tri, mid, preferred_element_type=F32)
                  + jnp.dot(tri, lo, preferred_element_type=F32))
    for h in heads:
        b_s[h] = bs[h]
        k_s[h] = ks[h]
        v_s[h] = vs[h]

    os_ = [lax.dot_general((qs[h] * jnp.exp(bs[h])).astype(BF16), sts[h].astype(BF16), NT_DIMS,
                           preferred_element_type=F32) for h in heads]

    for j in range(nsub - 1):
        r0, r1 = j * SUB, (j + 1) * SUB
        attn = []
        for h in heads:
            e_j = b_s[h, r1 - 1:r1, :]
            kt = (ks[h][r0:r1] * jnp.exp(e_j - bs[h][r0:r1])).astype(BF16)
            qt = (qs[h][r1:] * jnp.exp(bs[h][r1:] - e_j)).astype(BF16)
            attn.append(lax.dot_general(qt, kt, NT_DIMS, preferred_element_type=F32))
        for h in heads:
            upd = jnp.dot(attn[h].astype(BF16), vs[h][r0:r1].astype(BF16), preferred_element_type=F32)
            os_[h] = jnp.concatenate([os_[h][:r1], os_[h][r1:] + upd], axis=0)

    new_sts = []
    for h in heads:
        b_last = b_s[h, CHUNK - 1:CHUNK, :]
        kh = (ks[h] * jnp.exp(b_last - bs[h])).astype(BF16)
        new_sts.append(sts[h] * jnp.exp(b_last)
                       + lax.dot_general(vs[h].astype(BF16), kh, TN_DIMS, preferred_element_type=F32))

    for h in heads:
        b, q = bs[h], qs[h]
        pieces = []
        for j in range(nsub):
            r0 = j * SUB
            for half in range(SUB // SUBLANES):
                t0 = r0 + half * SUBLANES
                bq = b[t0:r0 + SUB]
                qq = q[t0:r0 + SUB]
                trow = lax.broadcasted_iota(jnp.int32, bq.shape, 0)
                acc = jnp.zeros(bq.shape, F32)
                for sl in range(SUBLANES):
                    s = t0 + sl
                    d = jnp.where(trow >= sl, bq - b_s[h, s:s + 1, :], -jnp.inf)
                    p = jnp.exp(d) * (qq * k_s[h, s:s + 1, :])
                    acc = acc + jnp.sum(p, axis=1, keepdims=True) * v_s[h, s:s + 1, :]
                if half == 0:
                    first = acc
                else:
                    pieces.append(first[:half * SUBLANES])
                    pieces.append(first[half * SUBLANES:] + acc)
        os_[h] = os_[h] + jnp.concatenate(pieces, axis=0)
    return os_, new_sts


def _hgrn_kernel(x_ref, w_ref, lb_ref, gn_ref, a_ref, st_ref, q_s, k_s, v_s, lf_s, o_s, bc_s, kc_s, vc_s, *, tc):
    @pl.when(pl.program_id(1) == 0)
    def _():
        st_ref[...] = jnp.zeros(st_ref.shape, F32)

    d = D_MODEL
    xb = x_ref[...].astype(BF16)
    lb = lb_ref[...]
    hq = jnp.dot(xb, w_ref[:, 0:d], preferred_element_type=F32)
    q = hq * _sigmoid(hq)
    hf = jnp.dot(xb, w_ref[:, d:2 * d], preferred_element_type=F32)
    fg = lb + (1.0 - lb) * _sigmoid(hf)
    k = 1.0 - fg
    lf = jnp.log(fg)
    v = jnp.dot(xb, w_ref[:, 2 * d:3 * d], preferred_element_type=F32)
    for h in range(HGRN_HEADS):
        sl = slice(h * HGRN_EXPAND, (h + 1) * HGRN_EXPAND)
        q_s[h] = q[:, sl]
        k_s[h] = k[:, sl]
        v_s[h] = v[:, sl]
        lf_s[h] = lf[:, sl]

    heads = range(HGRN_HEADS)

    def chunk_body(ci, carry):
        rows = pl.ds(pl.multiple_of(ci * CHUNK, CHUNK), CHUNK)
        outs, sts = _hgrn_chunk([q_s[h, rows, :] for h in heads], [k_s[h, rows, :] for h in heads],
                                [v_s[h, rows, :] for h in heads], [lf_s[h, rows, :] for h in heads],
                                [st_ref[h] for h in heads], bc_s, kc_s, vc_s)
        for h in heads:
            o_s[h, rows, :] = outs[h]
            st_ref[h] = sts[h]
        return carry

    lax.fori_loop(0, tc // CHUNK, chunk_body, 0)

    hg = jnp.dot(xb, w_ref[:, 3 * d:4 * d], preferred_element_type=F32)
    gate = hg * _sigmoid(hg)
    gn = gn_ref[...]
    for h in range(HGRN_HEADS):
        sl = slice(h * HGRN_EXPAND, (h + 1) * HGRN_EXPAND)
        o = o_s[h]
        ms = jnp.mean(o * o, axis=-1, keepdims=True)
        a_ref[:, sl] = (o * lax.rsqrt(ms + RMS_EPS) * gn * gate[:, sl]).astype(BF16)


def _hgrn(x2, w_in, lb, g_norm, batch, seq, tc):
    nt = seq // tc
    row = pl.BlockSpec((tc, D_MODEL), lambda b, t: (b * nt + t, 0))
    hm = lambda: pltpu.VMEM((HGRN_HEADS, tc, HGRN_EXPAND), F32)
    cm = lambda: pltpu.VMEM((HGRN_HEADS, CHUNK, HGRN_EXPAND), F32)
    return pl.pallas_call(
        functools.partial(_hgrn_kernel, tc=tc),
        grid=(batch, nt),
        in_specs=[row, _const_spec(w_in.shape), _const_spec(lb.shape), _const_spec(g_norm.shape)],
        out_specs=row,
        out_shape=jax.ShapeDtypeStruct((batch * seq, D_MODEL), BF16),
        scratch_shapes=[pltpu.VMEM((HGRN_HEADS, HGRN_EXPAND, HGRN_EXPAND), F32),
                        hm(), hm(), hm(), hm(), hm(), cm(), cm(), cm()],
        compiler_params=pltpu.CompilerParams(dimension_semantics=("arbitrary", "arbitrary"),
                                             vmem_limit_bytes=VMEM_LIMIT),
        name="hgrn2",
    )(x2, w_in, lb, g_norm)


def _rope_tables(positions, rot_dim, period):
    half = rot_dim // 2
    inv = ROPE_THETA ** (-jnp.arange(0, rot_dim, 2, dtype=F32) / rot_dim)
    ang = positions.astype(F32)[..., None] * inv
    cos, sin = jnp.cos(ang), jnp.sin(ang)
    rest = ang.shape[:-1] + (period - 2 * half,)
    c = jnp.concatenate([cos, cos, jnp.ones(rest, F32)], axis=-1)
    s = jnp.concatenate([-sin, sin, jnp.zeros(rest, F32)], axis=-1)
    reps = LANES // period
    c = jnp.tile(c, (1, 1, reps)).reshape(-1, LANES)
    s = jnp.tile(s, (1, 1, reps)).reshape(-1, LANES)
    return c, s


def _row(v):
    return v.reshape(1, -1).astype(F32)


def _dsa_layer(x2, positions, w_in, g_cq, w_uq, w_iq, g_ik, b_ik, batch, seq, tm, qt):
    cq, sq = _rope_tables(positions, ATT_ROT_DIM, ATT_HEAD_DIM)
    ci, si = _rope_tables(positions, IDX_ROT_DIM, IDX_DIM)
    w_in_p = jnp.pad(w_in, ((0, 0), (0, ATT_IN_PAD - w_in.shape[1]))).astype(BF16)
    q, k, vt, qi, ki, wit = _dsa_proj(x2, w_in_p, _row(g_cq), w_uq.astype(BF16), w_iq.astype(BF16),
                                      _row(g_ik), _row(b_ik), cq, sq, ci, si, tm)
    return _dsa_attn(q, qi, wit, k, vt, ki, batch, seq, qt)


def kernel(x, positions, att_w_in, att_g_cq, att_w_uq, att_w_iq, att_g_ik, att_b_ik, att_w_o,
           hgrn_lb_logits, hgrn_w_in, hgrn_g_norm, hgrn_w_o,
           ffn_w_gate, ffn_w_up, ffn_w_down, ln_g, ln_b):
    batch, seq, d = x.shape
    n = batch * seq
    tm = min(512, n)
    qt = min(256, seq)
    tc = min(256, seq)
    lb_all = jnp.cumsum(jax.nn.softmax(hgrn_lb_logits.astype(F32), axis=0), axis=0)
    lb_all = lb_all - lb_all[0]

    h = x.reshape(n, d)
    for layer in range(DEPTH):
        j = layer // 2
        if layer % 2 == 0:
            a = _dsa_layer(h, positions, att_w_in[j], att_g_cq[j], att_w_uq[j], att_w_iq[j],
                           att_g_ik[j], att_b_ik[j], batch, seq, tm, qt)
            w_o = att_w_o[j]
        else:
            a = _hgrn(h, hgrn_w_in[j].astype(BF16), _row(lb_all[layer]), _row(hgrn_g_norm[j]),
                      batch, seq, tc)
            w_o = hgrn_w_o[j]
        h = _post(a, h, w_o.astype(BF16), _row(ln_g[layer, 0]), _row(ln_b[layer, 0]),
                  ffn_w_gate[layer].astype(BF16), ffn_w_up[layer].astype(BF16),
                  ffn_w_down[layer].astype(BF16), _row(ln_g[layer, 1]), _row(ln_b[layer, 1]), tm)
    return h.reshape(batch, seq, d)
```

```python
import functools

import jax
import jax.numpy as jnp
from jax import lax
from jax.experimental import pallas as pl
from jax.experimental.pallas import tpu as pltpu

F32 = jnp.float32
BF16 = jnp.bfloat16

D_MODEL = 1024
DEPTH = 2
ATT_HEADS = 16
ATT_HEAD_DIM = 64
ATT_KV_HEADS = 2
Q_LORA_RANK = 256
IDX_HEADS = 8
IDX_DIM = 128
INDEX_TOPK = 256
ATT_KV_DIM = ATT_KV_HEADS * ATT_HEAD_DIM
HGRN_EXPAND = 128
HGRN_HEADS = D_MODEL // HGRN_EXPAND
D_FF = 2816
ROPE_THETA = 500000.0
ATT_ROT_DIM = ATT_HEAD_DIM // 4
IDX_ROT_DIM = IDX_DIM // 4
DEEPNORM_ALPHA = (2 * DEPTH) ** 0.25
LN_EPS = 1e-5
RMS_EPS = 1e-6

LANES = 128
SUBLANES = 8
VMEM_LIMIT = 56 * 1024 * 1024
ATT_IN_PAD = 768
MASKED = -1e30
LOG2E = 1.4426950408889634
NT_DIMS = (((1,), (1,)), ((), ()))
TN_DIMS = (((0,), (0,)), ((), ()))


def _const_spec(shape):
    nd = len(shape)
    return pl.BlockSpec(shape, lambda *_: (0,) * nd, pipeline_mode=pl.Buffered(1))


def _layer_norm(x, g, b):
    mu = jnp.mean(x, axis=-1, keepdims=True)
    xc = x - mu
    var = jnp.mean(xc * xc, axis=-1, keepdims=True)
    return xc * lax.rsqrt(var + LN_EPS) * g + b


def _sigmoid(x):
    return 1.0 / (1.0 + jnp.exp(-x))


def _tree(op, xs):
    while len(xs) > 1:
        xs = [op(xs[i], xs[i + 1]) for i in range(0, len(xs) - 1, 2)] + ([xs[-1]] if len(xs) % 2 else [])
    return xs[0]


def _rope(t, c, s, half, period):
    lane = lax.broadcasted_iota(jnp.int32, t.shape, 1) % period
    up = pltpu.roll(t, LANES - half, 1)
    dn = pltpu.roll(t, half, 1)
    return t * c + jnp.where(lane < half, up, dn) * s


def _dsa_proj_kernel(x_ref, w_in_ref, g_cq_ref, w_uq_ref, w_iq_ref, g_ik_ref, b_ik_ref,
                     cq_ref, sq_ref, ci_ref, si_ref,
                     q_ref, k_ref, vt_ref, qi_ref, ki_ref, wit_ref):
    xb = x_ref[...].astype(BF16)
    h = jnp.dot(xb, w_in_ref[...], preferred_element_type=F32)
    c = h[:, :Q_LORA_RANK]
    ms = jnp.mean(c * c, axis=-1, keepdims=True)
    cb = (c * lax.rsqrt(ms + RMS_EPS) * g_cq_ref[...]).astype(BF16)
    q = jnp.dot(cb, w_uq_ref[...], preferred_element_type=F32)
    qi = jnp.dot(cb, w_iq_ref[...], preferred_element_type=F32)
    cq, sq, ci, si = cq_ref[...], sq_ref[...], ci_ref[...], si_ref[...]
    scale = ATT_HEAD_DIM ** -0.5 * LOG2E
    for j in range(D_MODEL // LANES):
        sl = slice(j * LANES, (j + 1) * LANES)
        q_ref[:, sl] = (_rope(q[:, sl], cq, sq, ATT_ROT_DIM // 2, ATT_HEAD_DIM) * scale).astype(BF16)
        qi_ref[:, sl] = _rope(qi[:, sl], ci, si, IDX_ROT_DIM // 2, IDX_DIM).astype(BF16)
    o = Q_LORA_RANK
    k_ref[...] = _rope(h[:, o:o + LANES], cq, sq, ATT_ROT_DIM // 2, ATT_HEAD_DIM).astype(BF16)
    vt = h[:, o + LANES:o + 2 * LANES].T
    tm = vt.shape[1]
    ones_row = jnp.where(lax.broadcasted_iota(jnp.int32, (ATT_HEAD_DIM, tm), 0) == 0, 1.0, 0.0)
    for g in range(ATT_KV_HEADS):
        vt_ref[g * LANES:g * LANES + ATT_HEAD_DIM, :] = vt[g * ATT_HEAD_DIM:(g + 1) * ATT_HEAD_DIM].astype(BF16)
        vt_ref[g * LANES + ATT_HEAD_DIM:(g + 1) * LANES, :] = ones_row.astype(BF16)
    ki = _layer_norm(h[:, o + 2 * LANES:o + 3 * LANES], g_ik_ref[...], b_ik_ref[...])
    ki_ref[...] = _rope(ki, ci, si, IDX_ROT_DIM // 2, IDX_DIM).astype(BF16)
    wi = h[:, o + 3 * LANES:o + 4 * LANES] * (IDX_HEADS ** -0.5 * IDX_DIM ** -0.5)
    wit_ref[...] = wi.T[:IDX_HEADS]


def _dsa_proj(x2, w_in_p, g_cq, w_uq, w_iq, g_ik, b_ik, cq, sq, ci, si, tm):
    n = x2.shape[0]
    row = lambda w: pl.BlockSpec((tm, w), lambda i: (i, 0))
    col = lambda r: pl.BlockSpec((r, tm), lambda i: (0, i))
    return pl.pallas_call(
        _dsa_proj_kernel,
        grid=(n // tm,),
        in_specs=[row(D_MODEL), _const_spec(w_in_p.shape), _const_spec(g_cq.shape),
                  _const_spec(w_uq.shape), _const_spec(w_iq.shape),
                  _const_spec(g_ik.shape), _const_spec(b_ik.shape),
                  row(LANES), row(LANES), row(LANES), row(LANES)],
        out_specs=[row(D_MODEL), row(LANES), col(ATT_KV_HEADS * LANES), row(D_MODEL), row(LANES),
                   col(IDX_HEADS)],
        out_shape=[jax.ShapeDtypeStruct((n, D_MODEL), BF16), jax.ShapeDtypeStruct((n, LANES), BF16),
                   jax.ShapeDtypeStruct((ATT_KV_HEADS * LANES, n), BF16),
                   jax.ShapeDtypeStruct((n, D_MODEL), BF16),
                   jax.ShapeDtypeStruct((n, LANES), BF16), jax.ShapeDtypeStruct((IDX_HEADS, n), F32)],
        compiler_params=pltpu.CompilerParams(dimension_semantics=("arbitrary",),
                                             vmem_limit_bytes=VMEM_LIMIT),
        name="dsa_proj",
    )(x2, w_in_p, g_cq, w_uq, w_iq, g_ik, b_ik, cq, sq, ci, si)


LOOKAHEAD = 4
INT_MIN = -2 ** 31
KEY_NEG_INF = INT_MIN + 0x7FFFFF


def _key_to_float(ukey):
    key = jnp.maximum(ukey ^ jnp.int32(INT_MIN), jnp.int32(KEY_NEG_INF))
    bits = key ^ ((key >> 31) & jnp.int32(0x7FFFFFFF))
    return lax.bitcast_convert_type(bits, F32)


def _dsa_attn_kernel(q_ref, qi_ref, wit_ref, k_ref, vt_ref, ki_ref, o_ref, s_ref, m_ref, acc_ref, ot_ref,
                     *, qt, topk, seq):
    i = pl.program_id(1)
    nch = i + 1
    kc = qt
    krow = lax.broadcasted_iota(jnp.int32, (kc, qt), 0)
    qcol = lax.broadcasted_iota(jnp.int32, (kc, qt), 1)

    def chunk_rows(c):
        return pl.ds(pl.multiple_of(c * kc, kc), kc)

    wit = wit_ref[...]

    def score_chunk(c, carry):
        kblk = ki_ref[chunk_rows(c), :]
        acc = None
        for h in range(IDX_HEADS):
            r = lax.dot_general(kblk, qi_ref[:, h * IDX_DIM:(h + 1) * IDX_DIM], NT_DIMS,
                                preferred_element_type=F32)
            contrib = jnp.maximum(r, 0.0) * wit[h:h + 1, :]
            acc = contrib if acc is None else acc + contrib
        s_ref[c] = acc
        return carry

    lax.fori_loop(0, nch, score_chunk, 0)
    s_ref[i] = jnp.where(krow <= qcol, s_ref[i], -jnp.inf)

    def count(pred):
        def body(c, acc):
            m = pred(s_ref[c], c)
            ones = [jnp.where(m[r * SUBLANES:(r + 1) * SUBLANES], 1.0, 0.0) for r in range(kc // SUBLANES)]
            return acc + _tree(jnp.add, ones)
        acc = lax.fori_loop(0, nch, body, jnp.zeros((SUBLANES, qt), F32))
        return jnp.sum(acc, axis=0, keepdims=True)

    def value_bit(it, carry):
        ans, cnt_ans = carry
        cand = ans | lax.shift_left(jnp.int32(1), 31 - it)
        cf = _key_to_float(cand)
        cnt = count(lambda blk, c: blk >= cf)
        keep = cnt >= topk
        return jnp.where(keep, cand, ans), jnp.where(keep, cnt, cnt_ans)

    all_keys = jnp.full((1, qt), kc, F32) * nch.astype(F32)
    ans, cnt_ge = lax.fori_loop(0, 32, value_bit, (jnp.zeros((1, qt), jnp.int32), all_keys))
    tau = _key_to_float(ans)
    cnt_gt = count(lambda blk, c: blk > tau)
    need = topk - cnt_gt
    finite = tau > -jnp.inf
    excess = jnp.where((cnt_ge - cnt_gt > need) & finite, 1.0, 0.0)

    def index_cut():
        nbits = (seq - 1).bit_length()

        def index_bit(it, p):
            cand = p | lax.shift_left(jnp.int32(1), nbits - 1 - it)
            cnt = count(lambda blk, c: (blk == tau) & (krow + c * kc < cand))
            return jnp.where(cnt < need, cand, p)
        return lax.fori_loop(0, nbits, index_bit, jnp.zeros((1, qt), jnp.int32))

    cut = lax.cond(jnp.max(excess) > 0.0, index_cut, lambda: jnp.full((1, qt), seq, jnp.int32))
    cut = jnp.where(finite, cut, -1)

    def write_bias(c, carry):
        blk = s_ref[c]
        sel = (blk > tau) | ((blk == tau) & (krow + c * kc <= cut))
        s_ref[c] = jnp.where(sel, 0.0, MASKED)
        return carry

    lax.fori_loop(0, nch, write_bias, 0)

    group = ATT_HEADS // ATT_KV_HEADS
    m_ref[...] = jnp.full(m_ref.shape, MASKED, F32)
    acc_ref[...] = jnp.zeros(acc_ref.shape, F32)
    def attend(c, carry):
        bias = s_ref[c]
        kgs = [k_ref[chunk_rows(c), g * ATT_HEAD_DIM:(g + 1) * ATT_HEAD_DIM] for g in range(ATT_KV_HEADS)]
        vtgs = [vt_ref[g * LANES:(g + 1) * LANES, chunk_rows(c)] for g in range(ATT_KV_HEADS)]
        def logits(h):
            return lax.dot_general(kgs[h // group], q_ref[:, h * ATT_HEAD_DIM:(h + 1) * ATT_HEAD_DIM],
                                   NT_DIMS, preferred_element_type=F32) + bias

        sts = {h: logits(h) for h in range(LOOKAHEAD)}
        for h in range(ATT_HEADS):
            st = sts.pop(h)
            mx = _tree(jnp.maximum, [st[r * SUBLANES:(r + 1) * SUBLANES] for r in range(kc // SUBLANES)])
            m_old = m_ref[h]
            m_new = jnp.maximum(m_old, jnp.max(mx, axis=0, keepdims=True))
            m_ref[h] = m_new
            p = jnp.exp2(st - m_new).astype(BF16)
            if h + LOOKAHEAD < ATT_HEADS:
                sts[h + LOOKAHEAD] = logits(h + LOOKAHEAD)
            acc_ref[h] = jnp.exp2(m_old - m_new) * acc_ref[h] + jnp.dot(
                vtgs[h // group], p, preferred_element_type=F32)
        return carry

    lax.fori_loop(0, nch, attend, 0)

    for h in range(ATT_HEADS):
        acc = acc_ref[h]
        ot_ref[h * ATT_HEAD_DIM:(h + 1) * ATT_HEAD_DIM, :] = (
            acc[:ATT_HEAD_DIM] * (1.0 / acc[ATT_HEAD_DIM:ATT_HEAD_DIM + 1]))
    o_ref[...] = ot_ref[...].T.astype(BF16)


def _dsa_attn(q, qi, wit, k, vt, ki, batch, seq, qt):
    nq = seq // qt
    topk = min(INDEX_TOPK, seq // 4)
    qrow = lambda w: pl.BlockSpec((qt, w), lambda b, i: (b * nq + i, 0))
    kv = pl.BlockSpec((seq, LANES), lambda b, i: (b, 0))
    return pl.pallas_call(
        functools.partial(_dsa_attn_kernel, qt=qt, topk=topk, seq=seq),
        grid=(batch, nq),
        in_specs=[qrow(D_MODEL), qrow(D_MODEL),
                  pl.BlockSpec((IDX_HEADS, qt), lambda b, i: (0, b * nq + i)),
                  kv, pl.BlockSpec((ATT_KV_HEADS * LANES, seq), lambda b, i: (0, b)), kv],
        out_specs=qrow(D_MODEL),
        out_shape=jax.ShapeDtypeStruct((batch * seq, D_MODEL), BF16),
        scratch_shapes=[pltpu.VMEM((nq, qt, qt), F32),
                        pltpu.VMEM((ATT_HEADS, 1, qt), F32),
                        pltpu.VMEM((ATT_HEADS, LANES, qt), F32),
                        pltpu.VMEM((D_MODEL, qt), F32)],
        compiler_params=pltpu.CompilerParams(dimension_semantics=("arbitrary", "arbitrary"),
                                             vmem_limit_bytes=VMEM_LIMIT),
        name="dsa_attn",
    )(q, qi, wit, k, vt, ki)


FF_CHUNKS = ((0, 768), (768, 768), (1536, 768), (2304, 512))


def _post_kernel(a_ref, x_ref, wo_ref, g1_ref, b1_ref, wg_ref, wu_ref, wd_ref, g2_ref, b2_ref, o_ref):
    y = jnp.dot(a_ref[...], wo_ref[...], preferred_element_type=F32)
    h = _layer_norm(DEEPNORM_ALPHA * x_ref[...] + y, g1_ref[...], b1_ref[...])
    hb = h.astype(BF16)
    f = jnp.zeros(h.shape, F32)
    for start, size in FF_CHUNKS:
        gate = jnp.dot(hb, wg_ref[:, start:start + size], preferred_element_type=F32)
        up = jnp.dot(hb, wu_ref[:, start:start + size], preferred_element_type=F32)
        act = (gate * _sigmoid(gate) * up).astype(BF16)
        f = f + jnp.dot(act, wd_ref[start:start + size, :], preferred_element_type=F32)
    o_ref[...] = _layer_norm(DEEPNORM_ALPHA * h + f, g2_ref[...], b2_ref[...])


def _post(a, x2, wo, g1, b1, wg, wu, wd, g2, b2, tm):
    n = x2.shape[0]
    row = pl.BlockSpec((tm, D_MODEL), lambda i: (i, 0))
    return pl.pallas_call(
        _post_kernel,
        grid=(n // tm,),
        in_specs=[row, row] + [_const_spec(t.shape) for t in (wo, g1, b1, wg, wu, wd, g2, b2)],
        out_specs=row,
        out_shape=jax.ShapeDtypeStruct((n, D_MODEL), F32),
        compiler_params=pltpu.CompilerParams(dimension_semantics=("arbitrary",),
                                             vmem_limit_bytes=VMEM_LIMIT),
        name="post_ffn",
    )(a, x2, wo, g1, b1, wg, wu, wd, g2, b2)


CHUNK = 64
SUB = 16


def _split3(x):
    hi = x.astype(BF16)
    r1 = x - hi.astype(F32)
    mid = r1.astype(BF16)
    lo = (r1 - mid.astype(F32)).astype(BF16)
    return hi, mid, lo


def _hgrn_chunk(qs, ks, vs, lfs, sts, b_s, k_s, v_s):
    heads = range(len(qs))
    rowi = lax.broadcasted_iota(jnp.int32, (CHUNK, CHUNK), 0)
    coli = lax.broadcasted_iota(jnp.int32, (CHUNK, CHUNK), 1)
    tri = jnp.where(rowi >= coli, 1.0, 0.0).astype(BF16)
    nsub = CHUNK // SUB

    bs = []
    for h in heads:
        hi, mid, lo = _split3(lfs[h])
        b = (jnp.dot(tri, hi, preferred_element_type=F32) + jnp.dot(tri, mid, preferred_element_type=F32)
             + jnp.dot(tri, lo, preferred_element_type=F32))
        bs.append(b * LOG2E)
    for h in heads:
        b_s[h] = bs[h]
        k_s[h] = ks[h]
        v_s[h] = vs[h]

    os_ = [lax.dot_general((qs[h] * jnp.exp2(bs[h])).astype(BF16), sts[h].astype(BF16), NT_DIMS,
                           preferred_element_type=F32) for h in heads]

    for j in range(nsub - 1):
        r0, r1 = j * SUB, (j + 1) * SUB
        attn = []
        for h in heads:
            e_j = b_s[h, r1 - 1:r1, :]
            kt = (ks[h][r0:r1] * jnp.exp2(e_j - bs[h][r0:r1])).astype(BF16)
            qt = (qs[h][r1:] * jnp.exp2(bs[h][r1:] - e_j)).astype(BF16)
            attn.append(lax.dot_general(qt, kt, NT_DIMS, preferred_element_type=F32))
        for h in heads:
            upd = jnp.dot(attn[h].astype(BF16), vs[h][r0:r1].astype(BF16), preferred_element_type=F32)
            os_[h] = jnp.concatenate([os_[h][:r1], os_[h][r1:] + upd], axis=0)

    new_sts = []
    for h in heads:
        b_last = b_s[h, CHUNK - 1:CHUNK, :]
        kh = (ks[h] * jnp.exp2(b_last - bs[h])).astype(BF16)
        new_sts.append(sts[h] * jnp.exp2(b_last)
                       + lax.dot_general(vs[h].astype(BF16), kh, TN_DIMS, preferred_element_type=F32))

    for h in heads:
        b, q = bs[h], qs[h]
        pieces = []
        for j in range(nsub):
            r0 = j * SUB
            for half in range(SUB // SUBLANES):
                t0 = r0 + half * SUBLANES
                bq = b[t0:r0 + SUB]
                qq = q[t0:r0 + SUB]
                trow = lax.broadcasted_iota(jnp.int32, bq.shape, 0)
                acc = jnp.zeros(bq.shape, F32)
                for sl in range(SUBLANES):
                    s = t0 + sl
                    d = jnp.where(trow >= sl, bq - b_s[h, s:s + 1, :], -jnp.inf)
                    p = jnp.exp2(d) * (qq * k_s[h, s:s + 1, :])
                    acc = acc + jnp.sum(p, axis=1, keepdims=True) * v_s[h, s:s + 1, :]
                if half == 0:
                    first = acc
                else:
                    pieces.append(first[:half * SUBLANES])
                    pieces.append(first[half * SUBLANES:] + acc)
        os_[h] = os_[h] + jnp.concatenate(pieces, axis=0)
    return os_, new_sts


def _hgrn_kernel(x_ref, w_ref, lb_ref, gn_ref, a_ref, st_ref, q_s, k_s, v_s, lf_s, o_s, bc_s, kc_s, vc_s, *, tc):
    @pl.when(pl.program_id(1) == 0)
    def _():
        st_ref[...] = jnp.zeros(st_ref.shape, F32)

    d = D_MODEL
    xb = x_ref[...].astype(BF16)
    lb = lb_ref[...]
    hq = jnp.dot(xb, w_ref[:, 0:d], preferred_element_type=F32)
    q = hq * _sigmoid(hq)
    hf = jnp.dot(xb, w_ref[:, d:2 * d], preferred_element_type=F32)
    fg = lb + (1.0 - lb) * _sigmoid(hf)
    k = 1.0 - fg
    lf = jnp.log(fg)
    v = jnp.dot(xb, w_ref[:, 2 * d:3 * d], preferred_element_type=F32)
    for h in range(HGRN_HEADS):
        sl = slice(h * HGRN_EXPAND, (h + 1) * HGRN_EXPAND)
        q_s[h] = q[:, sl]
        k_s[h] = k[:, sl]
        v_s[h] = v[:, sl]
        lf_s[h] = lf[:, sl]

    heads = range(HGRN_HEADS)

    def chunk_body(ci, carry):
        rows = pl.ds(pl.multiple_of(ci * CHUNK, CHUNK), CHUNK)
        outs, sts = _hgrn_chunk([q_s[h, rows, :] for h in heads], [k_s[h, rows, :] for h in heads],
                                [v_s[h, rows, :] for h in heads], [lf_s[h, rows, :] for h in heads],
                                [st_ref[h] for h in heads], bc_s, kc_s, vc_s)
        for h in heads:
            o_s[h, rows, :] = outs[h]
            st_ref[h] = sts[h]
        return carry

    lax.fori_loop(0, tc // CHUNK, chunk_body, 0)

    hg = jnp.dot(xb, w_ref[:, 3 * d:4 * d], preferred_element_type=F32)
    gate = hg * _sigmoid(hg)
    gn = gn_ref[...]
    for h in range(HGRN_HEADS):
        sl = slice(h * HGRN_EXPAND, (h + 1) * HGRN_EXPAND)
        o = o_s[h]
        ms = jnp.mean(o * o, axis=-1, keepdims=True)
        a_ref[:, sl] = (o * lax.rsqrt(ms + RMS_EPS) * gn * gate[:, sl]).astype(BF16)


def _hgrn(x2, w_in, lb, g_norm, batch, seq, tc):
    nt = seq // tc
    row = pl.BlockSpec((tc, D_MODEL), lambda b, t: (b * nt + t, 0))
    hm = lambda: pltpu.VMEM((HGRN_HEADS, tc, HGRN_EXPAND), F32)
    cm = lambda: pltpu.VMEM((HGRN_HEADS, CHUNK, HGRN_EXPAND), F32)
    return pl.pallas_call(
        functools.partial(_hgrn_kernel, tc=tc),
        grid=(batch, nt),
        in_specs=[row, _const_spec(w_in.shape), _const_spec(lb.shape), _const_spec(g_norm.shape)],
        out_specs=row,
        out_shape=jax.ShapeDtypeStruct((batch * seq, D_MODEL), BF16),
        scratch_shapes=[pltpu.VMEM((HGRN_HEADS, HGRN_EXPAND, HGRN_EXPAND), F32),
                        hm(), hm(), hm(), hm(), hm(), cm(), cm(), cm()],
        compiler_params=pltpu.CompilerParams(dimension_semantics=("arbitrary", "arbitrary"),
                                             vmem_limit_bytes=VMEM_LIMIT),
        name="hgrn2",
    )(x2, w_in, lb, g_norm)


def _rope_tables(positions, rot_dim, period):
    half = rot_dim // 2
    inv = ROPE_THETA ** (-jnp.arange(0, rot_dim, 2, dtype=F32) / rot_dim)
    ang = positions.astype(F32)[..., None] * inv
    cos, sin = jnp.cos(ang), jnp.sin(ang)
    rest = ang.shape[:-1] + (period - 2 * half,)
    c = jnp.concatenate([cos, cos, jnp.ones(rest, F32)], axis=-1)
    s = jnp.concatenate([-sin, sin, jnp.zeros(rest, F32)], axis=-1)
    reps = LANES // period
    c = jnp.tile(c, (1, 1, reps)).reshape(-1, LANES)
    s = jnp.tile(s, (1, 1, reps)).reshape(-1, LANES)
    return c, s


def _row(v):
    return v.reshape(1, -1).astype(F32)


def _dsa_layer(x2, positions, w_in, g_cq, w_uq, w_iq, g_ik, b_ik, batch, seq, tm, qt):
    cq, sq = _rope_tables(positions, ATT_ROT_DIM, ATT_HEAD_DIM)
    ci, si = _rope_tables(positions, IDX_ROT_DIM, IDX_DIM)
    w_in_p = jnp.pad(w_in, ((0, 0), (0, ATT_IN_PAD - w_in.shape[1]))).astype(BF16)
    q, k, vt, qi, ki, wit = _dsa_proj(x2, w_in_p, _row(g_cq), w_uq.astype(BF16), w_iq.astype(BF16),
                                      _row(g_ik), _row(b_ik), cq, sq, ci, si, tm)
    return _dsa_attn(q, qi, wit, k, vt, ki, batch, seq, qt)


def kernel(x, positions, att_w_in, att_g_cq, att_w_uq, att_w_iq, att_g_ik, att_b_ik, att_w_o,
           hgrn_lb_logits, hgrn_w_in, hgrn_g_norm, hgrn_w_o,
           ffn_w_gate, ffn_w_up, ffn_w_down, ln_g, ln_b):
    batch, seq, d = x.shape
    n = batch * seq
    tm = min(512, n)
    qt = min(256, seq)
    tc = min(256, seq)
    lb_all = jnp.cumsum(jax.nn.softmax(hgrn_lb_logits.astype(F32), axis=0), axis=0)
    lb_all = lb_all - lb_all[0]

    h = x.reshape(n, d)
    for layer in range(DEPTH):
        j = layer // 2
        if layer % 2 == 0:
            a = _dsa_layer(h, positions, att_w_in[j], att_g_cq[j], att_w_uq[j], att_w_iq[j],
                           att_g_ik[j], att_b_ik[j], batch, seq, tm, qt)
            w_o = att_w_o[j]
        else:
            a = _hgrn(h, hgrn_w_in[j].astype(BF16), _row(lb_all[layer]), _row(hgrn_g_norm[j]),
                      batch, seq, tc)
            w_o = hgrn_w_o[j]
        h = _post(a, h, w_o.astype(BF16), _row(ln_g[layer, 0]), _row(ln_b[layer, 0]),
                  ffn_w_gate[layer].astype(BF16), ffn_w_up[layer].astype(BF16),
                  ffn_w_down[layer].astype(BF16), _row(ln_g[layer, 1]), _row(ln_b[layer, 1]), tm)
    return h.reshape(batch, seq, d)
```

```python
import functools

import jax
import numpy as np
import jax.numpy as jnp
from jax import lax
from jax.experimental import pallas as pl
from jax.experimental.pallas import tpu as pltpu

F32 = jnp.float32
BF16 = jnp.bfloat16

D_MODEL = 1024
DEPTH = 2
ATT_HEADS = 16
ATT_HEAD_DIM = 64
ATT_KV_HEADS = 2
Q_LORA_RANK = 256
IDX_HEADS = 8
IDX_DIM = 128
INDEX_TOPK = 256
ATT_KV_DIM = ATT_KV_HEADS * ATT_HEAD_DIM
HGRN_EXPAND = 128
HGRN_HEADS = D_MODEL // HGRN_EXPAND
D_FF = 2816
ROPE_THETA = 500000.0
ATT_ROT_DIM = ATT_HEAD_DIM // 4
IDX_ROT_DIM = IDX_DIM // 4
DEEPNORM_ALPHA = (2 * DEPTH) ** 0.25
LN_EPS = 1e-5
RMS_EPS = 1e-6

LANES = 128
SUBLANES = 8
VMEM_LIMIT = 56 * 1024 * 1024
ATT_IN_PAD = 768
MASKED = -1e30
LOG2E = 1.4426950408889634
NT_DIMS = (((1,), (1,)), ((), ()))
TN_DIMS = (((0,), (0,)), ((), ()))


def _const_spec(shape):
    nd = len(shape)
    return pl.BlockSpec(shape, lambda *_: (0,) * nd, pipeline_mode=pl.Buffered(1))


def _layer_norm(x, g, b):
    mu = jnp.mean(x, axis=-1, keepdims=True)
    xc = x - mu
    var = jnp.mean(xc * xc, axis=-1, keepdims=True)
    return xc * lax.rsqrt(var + LN_EPS) * g + b


def _sigmoid(x):
    return 1.0 / (1.0 + jnp.exp(-x))


def _tree(op, xs):
    while len(xs) > 1:
        xs = [op(xs[i], xs[i + 1]) for i in range(0, len(xs) - 1, 2)] + ([xs[-1]] if len(xs) % 2 else [])
    return xs[0]


def _rope(t, c, s, half, period):
    lane = lax.broadcasted_iota(jnp.int32, t.shape, 1) % period
    up = pltpu.roll(t, LANES - half, 1)
    dn = pltpu.roll(t, half, 1)
    return t * c + jnp.where(lane < half, up, dn) * s


def _rope_mix(t, t_rot, c, s):
    return t * c + t_rot * s


def _dsa_proj_kernel(x_ref, w_in_ref, g_cq_ref, w_uq_ref, w_uqr_ref, w_iq_ref, w_iqr_ref, g_ik_ref, b_ik_ref,
                     cq_ref, sq_ref, ci_ref, si_ref,
                     q_ref, k_ref, vt_ref, qi_ref, ki_ref, wit_ref):
    xb = x_ref[...].astype(BF16)
    h = jnp.dot(xb, w_in_ref[...], preferred_element_type=F32)
    c = h[:, :Q_LORA_RANK]
    ms = jnp.mean(c * c, axis=-1, keepdims=True)
    cb = (c * lax.rsqrt(ms + RMS_EPS) * g_cq_ref[...]).astype(BF16)
    cq, sq, ci, si = cq_ref[...], sq_ref[...], ci_ref[...], si_ref[...]
    scale = ATT_HEAD_DIM ** -0.5 * LOG2E
    q = jnp.dot(cb, w_uq_ref[...], preferred_element_type=F32)
    q_rot = jnp.dot(cb, w_uqr_ref[...], preferred_element_type=F32)
    qi = jnp.dot(cb, w_iq_ref[...], preferred_element_type=F32)
    qi_rot = jnp.dot(cb, w_iqr_ref[...], preferred_element_type=F32)
    for j in range(D_MODEL // LANES):
        sl = slice(j * LANES, (j + 1) * LANES)
        q_ref[:, sl] = (_rope_mix(q[:, sl], q_rot[:, sl], cq, sq) * scale).astype(BF16)
        qi_ref[:, sl] = _rope_mix(qi[:, sl], qi_rot[:, sl], ci, si).astype(BF16)
    o = Q_LORA_RANK
    k_ref[...] = _rope_mix(h[:, o:o + LANES], h[:, o + 4 * LANES:o + 5 * LANES], cq, sq).astype(BF16)
    vt = h[:, o + LANES:o + 2 * LANES].T
    tm = vt.shape[1]
    pad = VT_ROWS - ATT_HEAD_DIM
    ones_row = jnp.where(lax.broadcasted_iota(jnp.int32, (pad, tm), 0) == 0, 1.0, 0.0)
    for g in range(ATT_KV_HEADS):
        vt_ref[g * VT_ROWS:g * VT_ROWS + ATT_HEAD_DIM, :] = vt[g * ATT_HEAD_DIM:(g + 1) * ATT_HEAD_DIM].astype(BF16)
        vt_ref[g * VT_ROWS + ATT_HEAD_DIM:(g + 1) * VT_ROWS, :] = ones_row.astype(BF16)
    ki = _layer_norm(h[:, o + 2 * LANES:o + 3 * LANES], g_ik_ref[...], b_ik_ref[...])
    ki_ref[...] = _rope(ki, ci, si, IDX_ROT_DIM // 2, IDX_DIM).astype(BF16)
    wi = h[:, o + 3 * LANES:o + 4 * LANES] * (IDX_HEADS ** -0.5 * IDX_DIM ** -0.5)
    wit_ref[...] = wi.T[:IDX_HEADS]


def _dsa_proj(x2, w_in_p, g_cq, w_uq, w_uqr, w_iq, w_iqr, g_ik, b_ik, cq, sq, ci, si, tm):
    n = x2.shape[0]
    row = lambda w: pl.BlockSpec((tm, w), lambda i: (i, 0))
    col = lambda r: pl.BlockSpec((r, tm), lambda i: (0, i))
    return pl.pallas_call(
        _dsa_proj_kernel,
        grid=(n // tm,),
        in_specs=[row(D_MODEL), _const_spec(w_in_p.shape), _const_spec(g_cq.shape),
                  _const_spec(w_uq.shape), _const_spec(w_uqr.shape), _const_spec(w_iq.shape),
                  _const_spec(w_iqr.shape), _const_spec(g_ik.shape), _const_spec(b_ik.shape),
                  row(LANES), row(LANES), row(LANES), row(LANES)],
        out_specs=[row(D_MODEL), row(LANES), col(ATT_KV_HEADS * VT_ROWS), row(D_MODEL), row(LANES),
                   col(IDX_HEADS)],
        out_shape=[jax.ShapeDtypeStruct((n, D_MODEL), BF16), jax.ShapeDtypeStruct((n, LANES), BF16),
                   jax.ShapeDtypeStruct((ATT_KV_HEADS * VT_ROWS, n), BF16),
                   jax.ShapeDtypeStruct((n, D_MODEL), BF16),
                   jax.ShapeDtypeStruct((n, LANES), BF16), jax.ShapeDtypeStruct((IDX_HEADS, n), F32)],
        compiler_params=pltpu.CompilerParams(dimension_semantics=("arbitrary",),
                                             vmem_limit_bytes=VMEM_LIMIT),
        name="dsa_proj",
    )(x2, w_in_p, g_cq, w_uq, w_uqr, w_iq, w_iqr, g_ik, b_ik, cq, sq, ci, si)


VT_ROWS = 80
LOOKAHEAD = 4
INT_MIN = -2 ** 31
KEY_NEG_INF = INT_MIN + 0x7FFFFF


def _key_to_float(ukey):
    key = jnp.maximum(ukey ^ jnp.int32(INT_MIN), jnp.int32(KEY_NEG_INF))
    bits = key ^ ((key >> 31) & jnp.int32(0x7FFFFFFF))
    return lax.bitcast_convert_type(bits, F32)


def _high_half(x):
    bits = lax.bitcast_convert_type(x, jnp.int32) & jnp.int32(-65536)
    return lax.bitcast_convert_type(bits, F32).astype(BF16)


def _dsa_attn_kernel(q_ref, qi_ref, wit_ref, k_ref, vt_ref, ki_ref, o_ref, s_ref, s16_ref, m_ref, acc_ref, ot_ref,
                     *, qt, topk, seq):
    i = pl.program_id(1)
    nch = i + 1
    kc = qt
    krow = lax.broadcasted_iota(jnp.int32, (kc, qt), 0)
    qcol = lax.broadcasted_iota(jnp.int32, (kc, qt), 1)

    def chunk_rows(c):
        return pl.ds(pl.multiple_of(c * kc, kc), kc)

    wit = wit_ref[...]

    def score_chunk(c, carry):
        kblk = ki_ref[chunk_rows(c), :]
        acc = None
        for h in range(IDX_HEADS):
            r = lax.dot_general(kblk, qi_ref[:, h * IDX_DIM:(h + 1) * IDX_DIM], NT_DIMS,
                                preferred_element_type=F32)
            contrib = jnp.maximum(r, 0.0) * wit[h:h + 1, :]
            acc = contrib if acc is None else acc + contrib
        s_ref[c] = acc
        s16_ref[c] = _high_half(acc)
        return carry

    lax.fori_loop(0, nch, score_chunk, 0)
    diag = jnp.where(krow <= qcol, s_ref[i], -jnp.inf)
    s_ref[i] = diag
    s16_ref[i] = _high_half(diag)

    def count(pred):
        def body(c, acc):
            m = pred(s_ref[c], c)
            ones = [jnp.where(m[r * SUBLANES:(r + 1) * SUBLANES], 1.0, 0.0) for r in range(kc // SUBLANES)]
            return acc + _tree(jnp.add, ones)
        acc = lax.fori_loop(0, nch, body, jnp.zeros((SUBLANES, qt), F32))
        return jnp.sum(acc, axis=0, keepdims=True)

    def count_high(cf16):
        rows16 = 2 * SUBLANES
        one, zero = jnp.ones((rows16, qt), BF16), jnp.zeros((rows16, qt), BF16)

        def body(c, acc):
            m = s16_ref[c] >= cf16
            ones = [jnp.where(m[r * rows16:(r + 1) * rows16], one, zero) for r in range(kc // rows16)]
            return acc + _tree(jnp.add, ones).astype(F32)
        acc = lax.fori_loop(0, nch, body, jnp.zeros((rows16, qt), F32))
        return jnp.sum(acc, axis=0, keepdims=True)

    def value_bit(it, carry, high):
        ans, cnt_ans = carry
        cand = ans | lax.shift_left(jnp.int32(1), 31 - it)
        cf = _key_to_float(cand)
        cnt = count_high(_high_half(cf)) if high else count(lambda blk, c: blk >= cf)
        keep = cnt >= topk
        return jnp.where(keep, cand, ans), jnp.where(keep, cnt, cnt_ans)

    all_keys = jnp.full((1, qt), kc, F32) * nch.astype(F32)
    carry = (jnp.zeros((1, qt), jnp.int32), all_keys)
    carry = lax.fori_loop(0, 16, functools.partial(value_bit, high=True), carry)
    ans, cnt_ge = lax.fori_loop(16, 32, functools.partial(value_bit, high=False), carry)
    tau = _key_to_float(ans)
    cnt_gt = count(lambda blk, c: blk > tau)
    need = topk - cnt_gt
    finite = tau > -jnp.inf
    excess = jnp.where((cnt_ge - cnt_gt > need) & finite, 1.0, 0.0)

    def index_cut():
        nbits = (seq - 1).bit_length()

        def index_bit(it, p):
            cand = p | lax.shift_left(jnp.int32(1), nbits - 1 - it)
            cnt = count(lambda blk, c: (blk == tau) & (krow + c * kc < cand))
            return jnp.where(cnt < need, cand, p)
        return lax.fori_loop(0, nbits, index_bit, jnp.zeros((1, qt), jnp.int32))

    cut = lax.cond(jnp.max(excess) > 0.0, index_cut, lambda: jnp.full((1, qt), seq, jnp.int32))
    cut = jnp.where(finite, cut, -1)

    def write_bias(c, carry):
        blk = s_ref[c]
        sel = (blk > tau) | ((blk == tau) & (krow + c * kc <= cut))
        s_ref[c] = jnp.where(sel, 0.0, MASKED)
        return carry

    lax.fori_loop(0, nch, write_bias, 0)

    group = ATT_HEADS // ATT_KV_HEADS
    m_ref[...] = jnp.full(m_ref.shape, MASKED, F32)
    acc_ref[...] = jnp.zeros(acc_ref.shape, F32)
    def attend(c, carry):
        bias = s_ref[c]
        kgs = [k_ref[chunk_rows(c), g * ATT_HEAD_DIM:(g + 1) * ATT_HEAD_DIM] for g in range(ATT_KV_HEADS)]
        vtgs = [vt_ref[g * VT_ROWS:(g + 1) * VT_ROWS, chunk_rows(c)] for g in range(ATT_KV_HEADS)]
        def logits(h):
            return lax.dot_general(kgs[h // group], q_ref[:, h * ATT_HEAD_DIM:(h + 1) * ATT_HEAD_DIM],
                                   NT_DIMS, preferred_element_type=F32) + bias

        sts = {h: logits(h) for h in range(LOOKAHEAD)}
        for h in range(ATT_HEADS):
            st = sts.pop(h)
            mx = _tree(jnp.maximum, [st[r * SUBLANES:(r + 1) * SUBLANES] for r in range(kc // SUBLANES)])
            m_old = m_ref[h]
            m_new = jnp.maximum(m_old, jnp.max(mx, axis=0, keepdims=True))
            m_ref[h] = m_new
            p = jnp.exp2(st - m_new).astype(BF16)
            if h + LOOKAHEAD < ATT_HEADS:
                sts[h + LOOKAHEAD] = logits(h + LOOKAHEAD)
            acc_ref[h] = jnp.exp2(m_old - m_new) * acc_ref[h] + jnp.dot(
                vtgs[h // group], p, preferred_element_type=F32)
        return carry

    lax.fori_loop(0, nch, attend, 0)

    for h in range(ATT_HEADS):
        acc = acc_ref[h]
        ot_ref[h * ATT_HEAD_DIM:(h + 1) * ATT_HEAD_DIM, :] = (
            acc[:ATT_HEAD_DIM] * (1.0 / acc[ATT_HEAD_DIM:ATT_HEAD_DIM + 1]))
    o_ref[...] = ot_ref[...].T.astype(BF16)


def _dsa_attn(q, qi, wit, k, vt, ki, batch, seq, qt):
    nq = seq // qt
    topk = min(INDEX_TOPK, seq // 4)
    qrow = lambda w: pl.BlockSpec((qt, w), lambda b, i: (b * nq + i, 0))
    kv = pl.BlockSpec((seq, LANES), lambda b, i: (b, 0))
    return pl.pallas_call(
        functools.partial(_dsa_attn_kernel, qt=qt, topk=topk, seq=seq),
        grid=(batch, nq),
        in_specs=[qrow(D_MODEL), qrow(D_MODEL),
                  pl.BlockSpec((IDX_HEADS, qt), lambda b, i: (0, b * nq + i)),
                  kv, pl.BlockSpec((ATT_KV_HEADS * VT_ROWS, seq), lambda b, i: (0, b)), kv],
        out_specs=qrow(D_MODEL),
        out_shape=jax.ShapeDtypeStruct((batch * seq, D_MODEL), BF16),
        scratch_shapes=[pltpu.VMEM((nq, qt, qt), F32),
                        pltpu.VMEM((nq, qt, qt), BF16),
                        pltpu.VMEM((ATT_HEADS, 1, qt), F32),
                        pltpu.VMEM((ATT_HEADS, VT_ROWS, qt), F32),
                        pltpu.VMEM((D_MODEL, qt), F32)],
        compiler_params=pltpu.CompilerParams(dimension_semantics=("arbitrary", "arbitrary"),
                                             vmem_limit_bytes=VMEM_LIMIT),
        name="dsa_attn",
    )(q, qi, wit, k, vt, ki)


FF_CHUNKS = ((0, 768), (768, 768), (1536, 768), (2304, 512))


def _post_kernel(a_ref, x_ref, wo_ref, g1_ref, b1_ref, wg_ref, wu_ref, wd_ref, g2_ref, b2_ref, o_ref):
    y = jnp.dot(a_ref[...], wo_ref[...], preferred_element_type=F32)
    h = _layer_norm(DEEPNORM_ALPHA * x_ref[...] + y, g1_ref[...], b1_ref[...])
    hb = h.astype(BF16)
    f = jnp.zeros(h.shape, F32)
    for start, size in FF_CHUNKS:
        gate = jnp.dot(hb, wg_ref[:, start:start + size], preferred_element_type=F32)
        up = jnp.dot(hb, wu_ref[:, start:start + size], preferred_element_type=F32)
        act = (gate * _sigmoid(gate) * up).astype(BF16)
        f = f + jnp.dot(act, wd_ref[start:start + size, :], preferred_element_type=F32)
    o_ref[...] = _layer_norm(DEEPNORM_ALPHA * h + f, g2_ref[...], b2_ref[...])


def _post(a, x2, wo, g1, b1, wg, wu, wd, g2, b2, tm):
    n = x2.shape[0]
    row = pl.BlockSpec((tm, D_MODEL), lambda i: (i, 0))
    return pl.pallas_call(
        _post_kernel,
        grid=(n // tm,),
        in_specs=[row, row] + [_const_spec(t.shape) for t in (wo, g1, b1, wg, wu, wd, g2, b2)],
        out_specs=row,
        out_shape=jax.ShapeDtypeStruct((n, D_MODEL), F32),
        compiler_params=pltpu.CompilerParams(dimension_semantics=("arbitrary",),
                                             vmem_limit_bytes=VMEM_LIMIT),
        name="post_ffn",
    )(a, x2, wo, g1, b1, wg, wu, wd, g2, b2)


CHUNK = 64
SUB = 16


def _split3(x):
    hi = x.astype(BF16)
    r1 = x - hi.astype(F32)
    mid = r1.astype(BF16)
    lo = (r1 - mid.astype(F32)).astype(BF16)
    return hi, mid, lo


def _hgrn_chunk(qs, ks, vs, lfs, sts, b_s, k_s, v_s):
    heads = range(len(qs))
    rowi = lax.broadcasted_iota(jnp.int32, (CHUNK, CHUNK), 0)
    coli = lax.broadcasted_iota(jnp.int32, (CHUNK, CHUNK), 1)
    tri = jnp.where(rowi >= coli, 1.0, 0.0).astype(BF16)
    nsub = CHUNK // SUB

    bs = []
    for h in heads:
        hi, mid, lo = _split3(lfs[h])
        b = (jnp.dot(tri, hi, preferred_element_type=F32) + jnp.dot(tri, mid, preferred_element_type=F32)
             + jnp.dot(tri, lo, preferred_element_type=F32))
        bs.append(b * LOG2E)
    for h in heads:
        b_s[h] = bs[h]
        k_s[h] = ks[h]
        v_s[h] = vs[h]

    os_ = [lax.dot_general((qs[h] * jnp.exp2(bs[h])).astype(BF16), sts[h].astype(BF16), NT_DIMS,
                           preferred_element_type=F32) for h in heads]

    for j in range(nsub - 1):
        r0, r1 = j * SUB, (j + 1) * SUB
        attn = []
        for h in heads:
            e_j = b_s[h, r1 - 1:r1, :]
            kt = (ks[h][r0:r1] * jnp.exp2(e_j - bs[h][r0:r1])).astype(BF16)
            qt = (qs[h][r1:] * jnp.exp2(bs[h][r1:] - e_j)).astype(BF16)
            attn.append(lax.dot_general(qt, kt, NT_DIMS, preferred_element_type=F32))
        for h in heads:
            upd = jnp.dot(attn[h].astype(BF16), vs[h][r0:r1].astype(BF16), preferred_element_type=F32)
            os_[h] = jnp.concatenate([os_[h][:r1], os_[h][r1:] + upd], axis=0)

    new_sts = []
    for h in heads:
        b_last = b_s[h, CHUNK - 1:CHUNK, :]
        kh = (ks[h] * jnp.exp2(b_last - bs[h])).astype(BF16)
        new_sts.append(sts[h] * jnp.exp2(b_last)
                       + lax.dot_general(vs[h].astype(BF16), kh, TN_DIMS, preferred_element_type=F32))

    for h in heads:
        b, q = bs[h], qs[h]
        pieces = []
        for j in range(nsub):
            r0 = j * SUB
            for half in range(SUB // SUBLANES):
                t0 = r0 + half * SUBLANES
                bq = b[t0:r0 + SUB]
                qq = q[t0:r0 + SUB]
                trow = lax.broadcasted_iota(jnp.int32, bq.shape, 0)
                acc = jnp.zeros(bq.shape, F32)
                for sl in range(SUBLANES):
                    s = t0 + sl
                    d = jnp.where(trow >= sl, bq - b_s[h, s:s + 1, :], -jnp.inf)
                    p = jnp.exp2(d) * (qq * k_s[h, s:s + 1, :])
                    acc = acc + jnp.sum(p, axis=1, keepdims=True) * v_s[h, s:s + 1, :]
                if half == 0:
                    first = acc
                else:
                    pieces.append(first[:half * SUBLANES])
                    pieces.append(first[half * SUBLANES:] + acc)
        os_[h] = os_[h] + jnp.concatenate(pieces, axis=0)
    return os_, new_sts


def _hgrn_kernel(x_ref, w_ref, lb_ref, gn_ref, a_ref, st_ref, q_s, k_s, v_s, lf_s, o_s, bc_s, kc_s, vc_s, *, tc):
    @pl.when(pl.program_id(1) == 0)
    def _():
        st_ref[...] = jnp.zeros(st_ref.shape, F32)

    d = D_MODEL
    xb = x_ref[...].astype(BF16)
    lb = lb_ref[...]
    hq = jnp.dot(xb, w_ref[:, 0:d], preferred_element_type=F32)
    q = hq * _sigmoid(hq)
    hf = jnp.dot(xb, w_ref[:, d:2 * d], preferred_element_type=F32)
    fg = lb + (1.0 - lb) * _sigmoid(hf)
    k = 1.0 - fg
    lf = jnp.log(fg)
    v = jnp.dot(xb, w_ref[:, 2 * d:3 * d], preferred_element_type=F32)
    for h in range(HGRN_HEADS):
        sl = slice(h * HGRN_EXPAND, (h + 1) * HGRN_EXPAND)
        q_s[h] = q[:, sl]
        k_s[h] = k[:, sl]
        v_s[h] = v[:, sl]
        lf_s[h] = lf[:, sl]

    heads = range(HGRN_HEADS)

    def chunk_body(ci, carry):
        rows = pl.ds(pl.multiple_of(ci * CHUNK, CHUNK), CHUNK)
        outs, sts = _hgrn_chunk([q_s[h, rows, :] for h in heads], [k_s[h, rows, :] for h in heads],
                                [v_s[h, rows, :] for h in heads], [lf_s[h, rows, :] for h in heads],
                                [st_ref[h] for h in heads], bc_s, kc_s, vc_s)
        for h in heads:
            o_s[h, rows, :] = outs[h]
            st_ref[h] = sts[h]
        return carry

    lax.fori_loop(0, tc // CHUNK, chunk_body, 0)

    hg = jnp.dot(xb, w_ref[:, 3 * d:4 * d], preferred_element_type=F32)
    gate = hg * _sigmoid(hg)
    gn = gn_ref[...]
    for h in range(HGRN_HEADS):
        sl = slice(h * HGRN_EXPAND, (h + 1) * HGRN_EXPAND)
        o = o_s[h]
        ms = jnp.mean(o * o, axis=-1, keepdims=True)
        a_ref[:, sl] = (o * lax.rsqrt(ms + RMS_EPS) * gn * gate[:, sl]).astype(BF16)


def _hgrn(x2, w_in, lb, g_norm, batch, seq, tc):
    nt = seq // tc
    row = pl.BlockSpec((tc, D_MODEL), lambda b, t: (b * nt + t, 0))
    hm = lambda: pltpu.VMEM((HGRN_HEADS, tc, HGRN_EXPAND), F32)
    cm = lambda: pltpu.VMEM((HGRN_HEADS, CHUNK, HGRN_EXPAND), F32)
    return pl.pallas_call(
        functools.partial(_hgrn_kernel, tc=tc),
        grid=(batch, nt),
        in_specs=[row, _const_spec(w_in.shape), _const_spec(lb.shape), _const_spec(g_norm.shape)],
        out_specs=row,
        out_shape=jax.ShapeDtypeStruct((batch * seq, D_MODEL), BF16),
        scratch_shapes=[pltpu.VMEM((HGRN_HEADS, HGRN_EXPAND, HGRN_EXPAND), F32),
                        hm(), hm(), hm(), hm(), hm(), cm(), cm(), cm()],
        compiler_params=pltpu.CompilerParams(dimension_semantics=("arbitrary", "arbitrary"),
                                             vmem_limit_bytes=VMEM_LIMIT),
        name="hgrn2",
    )(x2, w_in, lb, g_norm)


def _rope_tables(positions, rot_dim, period):
    half = rot_dim // 2
    inv = ROPE_THETA ** (-jnp.arange(0, rot_dim, 2, dtype=F32) / rot_dim)
    ang = positions.astype(F32)[..., None] * inv
    cos, sin = jnp.cos(ang), jnp.sin(ang)
    rest = ang.shape[:-1] + (period - 2 * half,)
    c = jnp.concatenate([cos, cos, jnp.ones(rest, F32)], axis=-1)
    s = jnp.concatenate([-sin, sin, jnp.zeros(rest, F32)], axis=-1)
    reps = LANES // period
    c = jnp.tile(c, (1, 1, reps)).reshape(-1, LANES)
    s = jnp.tile(s, (1, 1, reps)).reshape(-1, LANES)
    return c, s


def _row(v):
    return v.reshape(1, -1).astype(F32)


def _rotate_half_columns(w, half, period):
    j = np.arange(w.shape[1])
    r = j % period
    src_col = np.where(r < half, j + half, np.where(r < 2 * half, j - half, j))
    return w[:, src_col]


def _dsa_layer(x2, positions, w_in, g_cq, w_uq, w_iq, g_ik, b_ik, batch, seq, tm, qt):
    cq, sq = _rope_tables(positions, ATT_ROT_DIM, ATT_HEAD_DIM)
    ci, si = _rope_tables(positions, IDX_ROT_DIM, IDX_DIM)
    w_k = w_in[:, Q_LORA_RANK:Q_LORA_RANK + ATT_KV_DIM]
    w_in_p = jnp.concatenate([jnp.pad(w_in, ((0, 0), (0, ATT_IN_PAD - w_in.shape[1]))),
                              _rotate_half_columns(w_k, ATT_ROT_DIM // 2, ATT_HEAD_DIM)], axis=1).astype(BF16)
    q, k, vt, qi, ki, wit = _dsa_proj(
        x2, w_in_p, _row(g_cq), w_uq.astype(BF16),
        _rotate_half_columns(w_uq, ATT_ROT_DIM // 2, ATT_HEAD_DIM).astype(BF16), w_iq.astype(BF16),
        _rotate_half_columns(w_iq, IDX_ROT_DIM // 2, IDX_DIM).astype(BF16),
        _row(g_ik), _row(b_ik), cq, sq, ci, si, tm)
    return _dsa_attn(q, qi, wit, k, vt, ki, batch, seq, qt)


def kernel(x, positions, att_w_in, att_g_cq, att_w_uq, att_w_iq, att_g_ik, att_b_ik, att_w_o,
           hgrn_lb_logits, hgrn_w_in, hgrn_g_norm, hgrn_w_o,
           ffn_w_gate, ffn_w_up, ffn_w_down, ln_g, ln_b):
    batch, seq, d = x.shape
    n = batch * seq
    tm = min(512, n)
    qt = min(256, seq)
    tc = min(256, seq)
    lb_all = jnp.cumsum(jax.nn.softmax(hgrn_lb_logits.astype(F32), axis=0), axis=0)
    lb_all = lb_all - lb_all[0]

    h = x.reshape(n, d)
    for layer in range(DEPTH):
        j = layer // 2
        if layer % 2 == 0:
            a = _dsa_layer(h, positions, att_w_in[j], att_g_cq[j], att_w_uq[j], att_w_iq[j],
                           att_g_ik[j], att_b_ik[j], batch, seq, tm, qt)
            w_o = att_w_o[j]
        else:
            a = _hgrn(h, hgrn_w_in[j].astype(BF16), _row(lb_all[layer]), _row(hgrn_g_norm[j]),
                      batch, seq, tc)
            w_o = hgrn_w_o[j]
        h = _post(a, h, w_o.astype(BF16), _row(ln_g[layer, 0]), _row(ln_b[layer, 0]),
                  ffn_w_gate[layer].astype(BF16), ffn_w_up[layer].astype(BF16),
                  ffn_w_down[layer].astype(BF16), _row(ln_g[layer, 1]), _row(ln_b[layer, 1]), tm)
    return h.reshape(batch, seq, d)
```

```python
import functools

import jax
import numpy as np
import jax.numpy as jnp
from jax import lax
from jax.experimental import pallas as pl
from jax.experimental.pallas import tpu as pltpu

F32 = jnp.float32
BF16 = jnp.bfloat16

D_MODEL = 1024
DEPTH = 2
ATT_HEADS = 16
ATT_HEAD_DIM = 64
ATT_KV_HEADS = 2
Q_LORA_RANK = 256
IDX_HEADS = 8
IDX_DIM = 128
INDEX_TOPK = 256
ATT_KV_DIM = ATT_KV_HEADS * ATT_HEAD_DIM
HGRN_EXPAND = 128
HGRN_HEADS = D_MODEL // HGRN_EXPAND
D_FF = 2816
ROPE_THETA = 500000.0
ATT_ROT_DIM = ATT_HEAD_DIM // 4
IDX_ROT_DIM = IDX_DIM // 4
DEEPNORM_ALPHA = (2 * DEPTH) ** 0.25
LN_EPS = 1e-5
RMS_EPS = 1e-6

LANES = 128
SUBLANES = 8
VMEM_LIMIT = 56 * 1024 * 1024
ATT_IN_PAD = 768
MASKED = -1e30
LOG2E = 1.4426950408889634
NT_DIMS = (((1,), (1,)), ((), ()))
TN_DIMS = (((0,), (0,)), ((), ()))


def _const_spec(shape):
    nd = len(shape)
    return pl.BlockSpec(shape, lambda *_: (0,) * nd, pipeline_mode=pl.Buffered(1))


def _layer_norm(x, g, b):
    mu = jnp.mean(x, axis=-1, keepdims=True)
    xc = x - mu
    var = jnp.mean(xc * xc, axis=-1, keepdims=True)
    return xc * lax.rsqrt(var + LN_EPS) * g + b


def _sigmoid(x):
    return 1.0 / (1.0 + jnp.exp(-x))


def _tree(op, xs):
    while len(xs) > 1:
        xs = [op(xs[i], xs[i + 1]) for i in range(0, len(xs) - 1, 2)] + ([xs[-1]] if len(xs) % 2 else [])
    return xs[0]


def _rope(t, c, s, half, period):
    lane = lax.broadcasted_iota(jnp.int32, t.shape, 1) % period
    up = pltpu.roll(t, LANES - half, 1)
    dn = pltpu.roll(t, half, 1)
    return t * c + jnp.where(lane < half, up, dn) * s


def _rope_mix(t, t_rot, c, s):
    return t * c + t_rot * s


def _dsa_proj_kernel(x_ref, w_in_ref, g_cq_ref, w_uq_ref, w_uqr_ref, w_iq_ref, w_iqr_ref, g_ik_ref, b_ik_ref,
                     cq_ref, sq_ref, ci_ref, si_ref,
                     q_ref, k_ref, vt_ref, qi_ref, ki_ref, wit_ref):
    xb = x_ref[...].astype(BF16)
    h = jnp.dot(xb, w_in_ref[...], preferred_element_type=F32)
    c = h[:, :Q_LORA_RANK]
    ms = jnp.mean(c * c, axis=-1, keepdims=True)
    cb = (c * lax.rsqrt(ms + RMS_EPS) * g_cq_ref[...]).astype(BF16)
    cq, sq, ci, si = cq_ref[...], sq_ref[...], ci_ref[...], si_ref[...]
    scale = ATT_HEAD_DIM ** -0.5 * LOG2E
    q = jnp.dot(cb, w_uq_ref[...], preferred_element_type=F32)
    q_rot = jnp.dot(cb, w_uqr_ref[...], preferred_element_type=F32)
    qi = jnp.dot(cb, w_iq_ref[...], preferred_element_type=F32)
    qi_rot = jnp.dot(cb, w_iqr_ref[...], preferred_element_type=F32)
    for j in range(D_MODEL // LANES):
        sl = slice(j * LANES, (j + 1) * LANES)
        q_ref[:, sl] = (_rope_mix(q[:, sl], q_rot[:, sl], cq, sq) * scale).astype(BF16)
        qi_ref[:, sl] = _rope_mix(qi[:, sl], qi_rot[:, sl], ci, si).astype(BF16)
    o = Q_LORA_RANK
    k_ref[...] = _rope_mix(h[:, o:o + LANES], h[:, o + 4 * LANES:o + 5 * LANES], cq, sq).astype(BF16)
    vt = h[:, o + LANES:o + 2 * LANES].T
    tm = vt.shape[1]
    pad = VT_ROWS - ATT_HEAD_DIM
    ones_row = jnp.where(lax.broadcasted_iota(jnp.int32, (pad, tm), 0) == 0, 1.0, 0.0)
    for g in range(ATT_KV_HEADS):
        vt_ref[g * VT_ROWS:g * VT_ROWS + ATT_HEAD_DIM, :] = vt[g * ATT_HEAD_DIM:(g + 1) * ATT_HEAD_DIM].astype(BF16)
        vt_ref[g * VT_ROWS + ATT_HEAD_DIM:(g + 1) * VT_ROWS, :] = ones_row.astype(BF16)
    ki = _layer_norm(h[:, o + 2 * LANES:o + 3 * LANES], g_ik_ref[...], b_ik_ref[...])
    ki_ref[...] = _rope(ki, ci, si, IDX_ROT_DIM // 2, IDX_DIM).astype(BF16)
    wi = h[:, o + 3 * LANES:o + 4 * LANES] * (IDX_HEADS ** -0.5 * IDX_DIM ** -0.5)
    wit_ref[...] = wi.T[:IDX_HEADS]


def _dsa_proj(x2, w_in_p, g_cq, w_uq, w_uqr, w_iq, w_iqr, g_ik, b_ik, cq, sq, ci, si, tm):
    n = x2.shape[0]
    row = lambda w: pl.BlockSpec((tm, w), lambda i: (i, 0))
    col = lambda r: pl.BlockSpec((r, tm), lambda i: (0, i))
    return pl.pallas_call(
        _dsa_proj_kernel,
        grid=(n // tm,),
        in_specs=[row(D_MODEL), _const_spec(w_in_p.shape), _const_spec(g_cq.shape),
                  _const_spec(w_uq.shape), _const_spec(w_uqr.shape), _const_spec(w_iq.shape),
                  _const_spec(w_iqr.shape), _const_spec(g_ik.shape), _const_spec(b_ik.shape),
                  row(LANES), row(LANES), row(LANES), row(LANES)],
        out_specs=[row(D_MODEL), row(LANES), col(ATT_KV_HEADS * VT_ROWS), row(D_MODEL), row(LANES),
                   col(IDX_HEADS)],
        out_shape=[jax.ShapeDtypeStruct((n, D_MODEL), BF16), jax.ShapeDtypeStruct((n, LANES), BF16),
                   jax.ShapeDtypeStruct((ATT_KV_HEADS * VT_ROWS, n), BF16),
                   jax.ShapeDtypeStruct((n, D_MODEL), BF16),
                   jax.ShapeDtypeStruct((n, LANES), BF16), jax.ShapeDtypeStruct((IDX_HEADS, n), F32)],
        compiler_params=pltpu.CompilerParams(dimension_semantics=("arbitrary",),
                                             vmem_limit_bytes=VMEM_LIMIT),
        name="dsa_proj",
    )(x2, w_in_p, g_cq, w_uq, w_uqr, w_iq, w_iqr, g_ik, b_ik, cq, sq, ci, si)


VT_ROWS = 80
BOUND_SLACK = 1.01
MIN_DENOM = 2.0 ** -60
LOOKAHEAD = 4
INT_MIN = -2 ** 31
KEY_NEG_INF = INT_MIN + 0x7FFFFF


def _key_to_float(ukey):
    key = jnp.maximum(ukey ^ jnp.int32(INT_MIN), jnp.int32(KEY_NEG_INF))
    bits = key ^ ((key >> 31) & jnp.int32(0x7FFFFFFF))
    return lax.bitcast_convert_type(bits, F32)


def _dsa_attn_kernel(q_ref, qi_ref, wit_ref, k_ref, vt_ref, ki_ref, o_ref, s_ref, m_ref, acc_ref, ot_ref,
                     kmax_ref,
                     *, qt, topk, seq):
    i = pl.program_id(1)
    nch = i + 1
    kc = qt
    krow = lax.broadcasted_iota(jnp.int32, (kc, qt), 0)
    qcol = lax.broadcasted_iota(jnp.int32, (kc, qt), 1)

    def chunk_rows(c):
        return pl.ds(pl.multiple_of(c * kc, kc), kc)

    wit = wit_ref[...]

    def score_chunk(c, carry):
        kblk = ki_ref[chunk_rows(c), :]
        acc = None
        for h in range(IDX_HEADS):
            r = lax.dot_general(kblk, qi_ref[:, h * IDX_DIM:(h + 1) * IDX_DIM], NT_DIMS,
                                preferred_element_type=F32)
            contrib = jnp.maximum(r, 0.0) * wit[h:h + 1, :]
            acc = contrib if acc is None else acc + contrib
        s_ref[c] = acc
        return carry

    lax.fori_loop(0, nch, score_chunk, 0)
    s_ref[i] = jnp.where(krow <= qcol, s_ref[i], -jnp.inf)

    def count(pred):
        def body(c, acc):
            m = pred(s_ref[c], c)
            ones = [jnp.where(m[r * SUBLANES:(r + 1) * SUBLANES], 1.0, 0.0) for r in range(kc // SUBLANES)]
            return acc + _tree(jnp.add, ones)
        acc = lax.fori_loop(0, nch, body, jnp.zeros((SUBLANES, qt), F32))
        return jnp.sum(acc, axis=0, keepdims=True)

    def value_bit(it, carry):
        ans, cnt_ans = carry
        cand = ans | lax.shift_left(jnp.int32(1), 31 - it)
        cf = _key_to_float(cand)
        cnt = count(lambda blk, c: blk >= cf)
        keep = cnt >= topk
        return jnp.where(keep, cand, ans), jnp.where(keep, cnt, cnt_ans)

    all_keys = jnp.full((1, qt), kc, F32) * nch.astype(F32)
    ans, cnt_ge = lax.fori_loop(0, 32, value_bit, (jnp.zeros((1, qt), jnp.int32), all_keys))
    tau = _key_to_float(ans)
    cnt_gt = count(lambda blk, c: blk > tau)
    need = topk - cnt_gt
    finite = tau > -jnp.inf
    excess = jnp.where((cnt_ge - cnt_gt > need) & finite, 1.0, 0.0)

    def index_cut():
        nbits = (seq - 1).bit_length()

        def index_bit(it, p):
            cand = p | lax.shift_left(jnp.int32(1), nbits - 1 - it)
            cnt = count(lambda blk, c: (blk == tau) & (krow + c * kc < cand))
            return jnp.where(cnt < need, cand, p)
        return lax.fori_loop(0, nbits, index_bit, jnp.zeros((1, qt), jnp.int32))

    cut = lax.cond(jnp.max(excess) > 0.0, index_cut, lambda: jnp.full((1, qt), seq, jnp.int32))
    cut = jnp.where(finite, cut, -1)

    def write_bias(c, carry):
        blk = s_ref[c]
        sel = (blk > tau) | ((blk == tau) & (krow + c * kc <= cut))
        s_ref[c] = jnp.where(sel, 0.0, MASKED)
        return carry

    lax.fori_loop(0, nch, write_bias, 0)

    group = ATT_HEADS // ATT_KV_HEADS

    def chunk_operands(c):
        kgs = [k_ref[chunk_rows(c), g * ATT_HEAD_DIM:(g + 1) * ATT_HEAD_DIM] for g in range(ATT_KV_HEADS)]
        vtgs = [vt_ref[g * VT_ROWS:(g + 1) * VT_ROWS, chunk_rows(c)] for g in range(ATT_KV_HEADS)]
        bias = s_ref[c]

        def logits(h):
            return lax.dot_general(kgs[h // group], q_ref[:, h * ATT_HEAD_DIM:(h + 1) * ATT_HEAD_DIM],
                                   NT_DIMS, preferred_element_type=F32) + bias
        return logits, vtgs

    def pipelined(logits, consume):
        sts = {h: logits(h) for h in range(LOOKAHEAD)}
        for h in range(ATT_HEADS):
            p = consume(h, sts.pop(h))
            if h + LOOKAHEAD < ATT_HEADS:
                sts[h + LOOKAHEAD] = logits(h + LOOKAHEAD)
            yield h, p

    @pl.when(i == 0)
    def _():
        kf = k_ref[...].astype(F32)
        sq = kf * kf
        for g in range(ATT_KV_HEADS):
            n2 = jnp.sum(sq[:, g * ATT_HEAD_DIM:(g + 1) * ATT_HEAD_DIM], axis=1, keepdims=True)
            kmax_ref[g] = jnp.max(n2)

    qf = q_ref[...].astype(F32)
    head_of_lane = lax.broadcasted_iota(jnp.int32, (ATT_HEADS, D_MODEL), 1) // ATT_HEAD_DIM
    head_sel = jnp.where(head_of_lane == lax.broadcasted_iota(jnp.int32, (ATT_HEADS, D_MODEL), 0), 1.0, 0.0)
    qn2 = lax.dot_general(head_sel.astype(BF16), (qf * qf).astype(BF16), NT_DIMS,
                          preferred_element_type=F32)
    for h in range(ATT_HEADS):
        m_ref[h] = jnp.sqrt(qn2[h:h + 1, :] * kmax_ref[h // group]) * BOUND_SLACK

    acc_ref[...] = jnp.zeros(acc_ref.shape, F32)

    def attend_fast(c, carry):
        logits, vtgs = chunk_operands(c)
        for h, p in pipelined(logits, lambda h, st: jnp.exp2(st - m_ref[h]).astype(BF16)):
            acc_ref[h] = acc_ref[h] + jnp.dot(vtgs[h // group], p, preferred_element_type=F32)
        return carry

    lax.fori_loop(0, nch, attend_fast, 0)
    denom = _tree(jnp.minimum, [acc_ref[h, ATT_HEAD_DIM:ATT_HEAD_DIM + 1, :] for h in range(ATT_HEADS)])

    @pl.when(jnp.logical_not(jnp.min(denom) >= MIN_DENOM))
    def _():
        m_ref[...] = jnp.full(m_ref.shape, MASKED, F32)
        acc_ref[...] = jnp.zeros(acc_ref.shape, F32)

        def attend(c, carry):
            logits, vtgs = chunk_operands(c)
            alphas = {}

            def softmax_step(h, st):
                mx = _tree(jnp.maximum, [st[r * SUBLANES:(r + 1) * SUBLANES] for r in range(kc // SUBLANES)])
                m_old = m_ref[h]
                m_new = jnp.maximum(m_old, jnp.max(mx, axis=0, keepdims=True))
                m_ref[h] = m_new
                alphas[h] = jnp.exp2(m_old - m_new)
                return jnp.exp2(st - m_new).astype(BF16)

            for h, p in pipelined(logits, softmax_step):
                acc_ref[h] = alphas[h] * acc_ref[h] + jnp.dot(vtgs[h // group], p, preferred_element_type=F32)
            return carry

        lax.fori_loop(0, nch, attend, 0)

    for h in range(ATT_HEADS):
        acc = acc_ref[h]
        ot_ref[h * ATT_HEAD_DIM:(h + 1) * ATT_HEAD_DIM, :] = (
            acc[:ATT_HEAD_DIM] * (1.0 / acc[ATT_HEAD_DIM:ATT_HEAD_DIM + 1]))
    o_ref[...] = ot_ref[...].T.astype(BF16)


def _dsa_attn(q, qi, wit, k, vt, ki, batch, seq, qt):
    nq = seq // qt
    topk = min(INDEX_TOPK, seq // 4)
    qrow = lambda w: pl.BlockSpec((qt, w), lambda b, i: (b * nq + i, 0))
    kv = pl.BlockSpec((seq, LANES), lambda b, i: (b, 0))
    return pl.pallas_call(
        functools.partial(_dsa_attn_kernel, qt=qt, topk=topk, seq=seq),
        grid=(batch, nq),
        in_specs=[qrow(D_MODEL), qrow(D_MODEL),
                  pl.BlockSpec((IDX_HEADS, qt), lambda b, i: (0, b * nq + i)),
                  kv, pl.BlockSpec((ATT_KV_HEADS * VT_ROWS, seq), lambda b, i: (0, b)), kv],
        out_specs=qrow(D_MODEL),
        out_shape=jax.ShapeDtypeStruct((batch * seq, D_MODEL), BF16),
        scratch_shapes=[pltpu.VMEM((nq, qt, qt), F32),
                        pltpu.VMEM((ATT_HEADS, 1, qt), F32),
                        pltpu.VMEM((ATT_HEADS, VT_ROWS, qt), F32),
                        pltpu.VMEM((D_MODEL, qt), F32),
                        pltpu.SMEM((ATT_KV_HEADS,), F32)],
        compiler_params=pltpu.CompilerParams(dimension_semantics=("arbitrary", "arbitrary"),
                                             vmem_limit_bytes=VMEM_LIMIT),
        name="dsa_attn",
    )(q, qi, wit, k, vt, ki)


FF_CHUNKS = ((0, 768), (768, 768), (1536, 768), (2304, 512))


def _post_kernel(a_ref, x_ref, wo_ref, g1_ref, b1_ref, wg_ref, wu_ref, wd_ref, g2_ref, b2_ref, o_ref):
    y = jnp.dot(a_ref[...], wo_ref[...], preferred_element_type=F32)
    h = _layer_norm(DEEPNORM_ALPHA * x_ref[...] + y, g1_ref[...], b1_ref[...])
    hb = h.astype(BF16)
    f = jnp.zeros(h.shape, F32)
    for start, size in FF_CHUNKS:
        gate = jnp.dot(hb, wg_ref[:, start:start + size], preferred_element_type=F32)
        up = jnp.dot(hb, wu_ref[:, start:start + size], preferred_element_type=F32)
        act = (gate * _sigmoid(gate) * up).astype(BF16)
        f = f + jnp.dot(act, wd_ref[start:start + size, :], preferred_element_type=F32)
    o_ref[...] = _layer_norm(DEEPNORM_ALPHA * h + f, g2_ref[...], b2_ref[...])


def _post(a, x2, wo, g1, b1, wg, wu, wd, g2, b2, tm):
    n = x2.shape[0]
    row = pl.BlockSpec((tm, D_MODEL), lambda i: (i, 0))
    return pl.pallas_call(
        _post_kernel,
        grid=(n // tm,),
        in_specs=[row, row] + [_const_spec(t.shape) for t in (wo, g1, b1, wg, wu, wd, g2, b2)],
        out_specs=row,
        out_shape=jax.ShapeDtypeStruct((n, D_MODEL), F32),
        compiler_params=pltpu.CompilerParams(dimension_semantics=("arbitrary",),
                                             vmem_limit_bytes=VMEM_LIMIT),
        name="post_ffn",
    )(a, x2, wo, g1, b1, wg, wu, wd, g2, b2)


CHUNK = 64
SUB = 16


def _split3(x):
    hi = x.astype(BF16)
    r1 = x - hi.astype(F32)
    mid = r1.astype(BF16)
    lo = (r1 - mid.astype(F32)).astype(BF16)
    return hi, mid, lo


def _hgrn_chunk(qs, ks, vs, lfs, sts, b_s, k_s, v_s):
    heads = range(len(qs))
    rowi = lax.broadcasted_iota(jnp.int32, (CHUNK, CHUNK), 0)
    coli = lax.broadcasted_iota(jnp.int32, (CHUNK, CHUNK), 1)
    tri = jnp.where(rowi >= coli, 1.0, 0.0).astype(BF16)
    nsub = CHUNK // SUB

    bs = []
    for h in heads:
        hi, mid, lo = _split3(lfs[h])
        b = (jnp.dot(tri, hi, preferred_element_type=F32) + jnp.dot(tri, mid, preferred_element_type=F32)
             + jnp.dot(tri, lo, preferred_element_type=F32))
        bs.append(b * LOG2E)
    for h in heads:
        b_s[h] = bs[h]
        k_s[h] = ks[h]
        v_s[h] = vs[h]

    os_ = [lax.dot_general((qs[h] * jnp.exp2(bs[h])).astype(BF16), sts[h].astype(BF16), NT_DIMS,
                           preferred_element_type=F32) for h in heads]

    for j in range(nsub - 1):
        r0, r1 = j * SUB, (j + 1) * SUB
        attn = []
        for h in heads:
            e_j = b_s[h, r1 - 1:r1, :]
            kt = (ks[h][r0:r1] * jnp.exp2(e_j - bs[h][r0:r1])).astype(BF16)
            qt = (qs[h][r1:] * jnp.exp2(bs[h][r1:] - e_j)).astype(BF16)
            attn.append(lax.dot_general(qt, kt, NT_DIMS, preferred_element_type=F32))
        for h in heads:
            upd = jnp.dot(attn[h].astype(BF16), vs[h][r0:r1].astype(BF16), preferred_element_type=F32)
            os_[h] = jnp.concatenate([os_[h][:r1], os_[h][r1:] + upd], axis=0)

    new_sts = []
    for h in heads:
        b_last = b_s[h, CHUNK - 1:CHUNK, :]
        kh = (ks[h] * jnp.exp2(b_last - bs[h])).astype(BF16)
        new_sts.append(sts[h] * jnp.exp2(b_last)
                       + lax.dot_general(vs[h].astype(BF16), kh, TN_DIMS, preferred_element_type=F32))

    for h in heads:
        b, q = bs[h], qs[h]
        pieces = []
        for j in range(nsub):
            r0 = j * SUB
            for half in range(SUB // SUBLANES):
                t0 = r0 + half * SUBLANES
                bq = b[t0:r0 + SUB]
                qq = q[t0:r0 + SUB]
                trow = lax.broadcasted_iota(jnp.int32, bq.shape, 0)
                acc = jnp.zeros(bq.shape, F32)
                for sl in range(SUBLANES):
                    s = t0 + sl
                    d = jnp.where(trow >= sl, bq - b_s[h, s:s + 1, :], -jnp.inf)
                    p = jnp.exp2(d) * (qq * k_s[h, s:s + 1, :])
                    acc = acc + jnp.sum(p, axis=1, keepdims=True) * v_s[h, s:s + 1, :]
                if half == 0:
                    first = acc
                else:
                    pieces.append(first[:half * SUBLANES])
                    pieces.append(first[half * SUBLANES:] + acc)
        os_[h] = os_[h] + jnp.concatenate(pieces, axis=0)
    return os_, new_sts


def _hgrn_kernel(x_ref, w_ref, lb_ref, gn_ref, a_ref, st_ref, q_s, k_s, v_s, lf_s, o_s, bc_s, kc_s, vc_s, *, tc):
    @pl.when(pl.program_id(1) == 0)
    def _():
        st_ref[...] = jnp.zeros(st_ref.shape, F32)

    d = D_MODEL
    xb = x_ref[...].astype(BF16)
    lb = lb_ref[...]
    hq = jnp.dot(xb, w_ref[:, 0:d], preferred_element_type=F32)
    q = hq * _sigmoid(hq)
    hf = jnp.dot(xb, w_ref[:, d:2 * d], preferred_element_type=F32)
    fg = lb + (1.0 - lb) * _sigmoid(hf)
    k = 1.0 - fg
    lf = jnp.log(fg)
    v = jnp.dot(xb, w_ref[:, 2 * d:3 * d], preferred_element_type=F32)
    for h in range(HGRN_HEADS):
        sl = slice(h * HGRN_EXPAND, (h + 1) * HGRN_EXPAND)
        q_s[h] = q[:, sl]
        k_s[h] = k[:, sl]
        v_s[h] = v[:, sl]
        lf_s[h] = lf[:, sl]

    heads = range(HGRN_HEADS)

    def chunk_body(ci, carry):
        rows = pl.ds(pl.multiple_of(ci * CHUNK, CHUNK), CHUNK)
        outs, sts = _hgrn_chunk([q_s[h, rows, :] for h in heads], [k_s[h, rows, :] for h in heads],
                                [v_s[h, rows, :] for h in heads], [lf_s[h, rows, :] for h in heads],
                                [st_ref[h] for h in heads], bc_s, kc_s, vc_s)
        for h in heads:
            o_s[h, rows, :] = outs[h]
            st_ref[h] = sts[h]
        return carry

    lax.fori_loop(0, tc // CHUNK, chunk_body, 0)

    hg = jnp.dot(xb, w_ref[:, 3 * d:4 * d], preferred_element_type=F32)
    gate = hg * _sigmoid(hg)
    gn = gn_ref[...]
    for h in range(HGRN_HEADS):
        sl = slice(h * HGRN_EXPAND, (h + 1) * HGRN_EXPAND)
        o = o_s[h]
        ms = jnp.mean(o * o, axis=-1, keepdims=True)
        a_ref[:, sl] = (o * lax.rsqrt(ms + RMS_EPS) * gn * gate[:, sl]).astype(BF16)


def _hgrn(x2, w_in, lb, g_norm, batch, seq, tc):
    nt = seq // tc
    row = pl.BlockSpec((tc, D_MODEL), lambda b, t: (b * nt + t, 0))
    hm = lambda: pltpu.VMEM((HGRN_HEADS, tc, HGRN_EXPAND), F32)
    cm = lambda: pltpu.VMEM((HGRN_HEADS, CHUNK, HGRN_EXPAND), F32)
    return pl.pallas_call(
        functools.partial(_hgrn_kernel, tc=tc),
        grid=(batch, nt),
        in_specs=[row, _const_spec(w_in.shape), _const_spec(lb.shape), _const_spec(g_norm.shape)],
        out_specs=row,
        out_shape=jax.ShapeDtypeStruct((batch * seq, D_MODEL), BF16),
        scratch_shapes=[pltpu.VMEM((HGRN_HEADS, HGRN_EXPAND, HGRN_EXPAND), F32),
                        hm(), hm(), hm(), hm(), hm(), cm(), cm(), cm()],
        compiler_params=pltpu.CompilerParams(dimension_semantics=("arbitrary", "arbitrary"),
                                             vmem_limit_bytes=VMEM_LIMIT),
        name="hgrn2",
    )(x2, w_in, lb, g_norm)


def _rope_tables(positions, rot_dim, period):
    half = rot_dim // 2
    inv = ROPE_THETA ** (-jnp.arange(0, rot_dim, 2, dtype=F32) / rot_dim)
    ang = positions.astype(F32)[..., None] * inv
    cos, sin = jnp.cos(ang), jnp.sin(ang)
    rest = ang.shape[:-1] + (period - 2 * half,)
    c = jnp.concatenate([cos, cos, jnp.ones(rest, F32)], axis=-1)
    s = jnp.concatenate([-sin, sin, jnp.zeros(rest, F32)], axis=-1)
    reps = LANES // period
    c = jnp.tile(c, (1, 1, reps)).reshape(-1, LANES)
    s = jnp.tile(s, (1, 1, reps)).reshape(-1, LANES)
    return c, s


def _row(v):
    return v.reshape(1, -1).astype(F32)


def _rotate_half_columns(w, half, period):
    j = np.arange(w.shape[1])
    r = j % period
    src_col = np.where(r < half, j + half, np.where(r < 2 * half, j - half, j))
    return w[:, src_col]


def _dsa_layer(x2, positions, w_in, g_cq, w_uq, w_iq, g_ik, b_ik, batch, seq, tm, qt):
    cq, sq = _rope_tables(positions, ATT_ROT_DIM, ATT_HEAD_DIM)
    ci, si = _rope_tables(positions, IDX_ROT_DIM, IDX_DIM)
    w_k = w_in[:, Q_LORA_RANK:Q_LORA_RANK + ATT_KV_DIM]
    w_in_p = jnp.concatenate([jnp.pad(w_in, ((0, 0), (0, ATT_IN_PAD - w_in.shape[1]))),
                              _rotate_half_columns(w_k, ATT_ROT_DIM // 2, ATT_HEAD_DIM)], axis=1).astype(BF16)
    q, k, vt, qi, ki, wit = _dsa_proj(
        x2, w_in_p, _row(g_cq), w_uq.astype(BF16),
        _rotate_half_columns(w_uq, ATT_ROT_DIM // 2, ATT_HEAD_DIM).astype(BF16), w_iq.astype(BF16),
        _rotate_half_columns(w_iq, IDX_ROT_DIM // 2, IDX_DIM).astype(BF16),
        _row(g_ik), _row(b_ik), cq, sq, ci, si, tm)
    return _dsa_attn(q, qi, wit, k, vt, ki, batch, seq, qt)


def kernel(x, positions, att_w_in, att_g_cq, att_w_uq, att_w_iq, att_g_ik, att_b_ik, att_w_o,
           hgrn_lb_logits, hgrn_w_in, hgrn_g_norm, hgrn_w_o,
           ffn_w_gate, ffn_w_up, ffn_w_down, ln_g, ln_b):
    batch, seq, d = x.shape
    n = batch * seq
    tm = min(512, n)
    qt = min(256, seq)
    tc = min(256, seq)
    lb_all = jnp.cumsum(jax.nn.softmax(hgrn_lb_logits.astype(F32), axis=0), axis=0)
    lb_all = lb_all - lb_all[0]

    h = x.reshape(n, d)
    for layer in range(DEPTH):
        j = layer // 2
        if layer % 2 == 0:
            a = _dsa_layer(h, positions, att_w_in[j], att_g_cq[j], att_w_uq[j], att_w_iq[j],
                           att_g_ik[j], att_b_ik[j], batch, seq, tm, qt)
            w_o = att_w_o[j]
        else:
            a = _hgrn(h, hgrn_w_in[j].astype(BF16), _row(lb_all[layer]), _row(hgrn_g_norm[j]),
                      batch, seq, tc)
            w_o = hgrn_w_o[j]
        h = _post(a, h, w_o.astype(BF16), _row(ln_g[layer, 0]), _row(ln_b[layer, 0]),
                  ffn_w_gate[layer].astype(BF16), ffn_w_up[layer].astype(BF16),
                  ffn_w_down[layer].astype(BF16), _row(ln_g[layer, 1]), _row(ln_b[layer, 1]), tm)
    return h.reshape(batch, seq, d)
```

```python
import functools

import jax
import numpy as np
import jax.numpy as jnp
from jax import lax
from jax.experimental import pallas as pl
from jax.experimental.pallas import tpu as pltpu

F32 = jnp.float32
BF16 = jnp.bfloat16

D_MODEL = 1024
DEPTH = 2
ATT_HEADS = 16
ATT_HEAD_DIM = 64
ATT_KV_HEADS = 2
Q_LORA_RANK = 256
IDX_HEADS = 8
IDX_DIM = 128
INDEX_TOPK = 256
ATT_KV_DIM = ATT_KV_HEADS * ATT_HEAD_DIM
HGRN_EXPAND = 128
HGRN_HEADS = D_MODEL // HGRN_EXPAND
D_FF = 2816
ROPE_THETA = 500000.0
ATT_ROT_DIM = ATT_HEAD_DIM // 4
IDX_ROT_DIM = IDX_DIM // 4
DEEPNORM_ALPHA = (2 * DEPTH) ** 0.25
LN_EPS = 1e-5
RMS_EPS = 1e-6

LANES = 128
SUBLANES = 8
VMEM_LIMIT = 56 * 1024 * 1024
ATT_IN_PAD = 768
MASKED = -1e30
LOG2E = 1.4426950408889634
NT_DIMS = (((1,), (1,)), ((), ()))
TN_DIMS = (((0,), (0,)), ((), ()))


def _const_spec(shape):
    nd = len(shape)
    return pl.BlockSpec(shape, lambda *_: (0,) * nd, pipeline_mode=pl.Buffered(1))


def _layer_norm(x, g, b):
    mu = jnp.mean(x, axis=-1, keepdims=True)
    xc = x - mu
    var = jnp.mean(xc * xc, axis=-1, keepdims=True)
    return xc * lax.rsqrt(var + LN_EPS) * g + b


def _sigmoid(x):
    return 1.0 / (1.0 + jnp.exp(-x))


def _tree(op, xs):
    while len(xs) > 1:
        xs = [op(xs[i], xs[i + 1]) for i in range(0, len(xs) - 1, 2)] + ([xs[-1]] if len(xs) % 2 else [])
    return xs[0]


def _rope(t, c, s, half, period):
    lane = lax.broadcasted_iota(jnp.int32, t.shape, 1) % period
    up = pltpu.roll(t, LANES - half, 1)
    dn = pltpu.roll(t, half, 1)
    return t * c + jnp.where(lane < half, up, dn) * s


def _rope_mix(t, t_rot, c, s):
    return t * c + t_rot * s


def _dsa_proj_kernel(x_ref, w_in_ref, g_cq_ref, w_uq_ref, w_uqr_ref, w_iq_ref, w_iqr_ref, g_ik_ref, b_ik_ref,
                     cq_ref, sq_ref, ci_ref, si_ref,
                     q_ref, k_ref, vt_ref, qi_ref, ki_ref, wit_ref):
    xb = x_ref[...].astype(BF16)
    h = jnp.dot(xb, w_in_ref[...], preferred_element_type=F32)
    c = h[:, :Q_LORA_RANK]
    ms = jnp.mean(c * c, axis=-1, keepdims=True)
    cb = (c * lax.rsqrt(ms + RMS_EPS) * g_cq_ref[...]).astype(BF16)
    cq, sq, ci, si = cq_ref[...], sq_ref[...], ci_ref[...], si_ref[...]
    scale = ATT_HEAD_DIM ** -0.5 * LOG2E
    q = jnp.dot(cb, w_uq_ref[...], preferred_element_type=F32)
    q_rot = jnp.dot(cb, w_uqr_ref[...], preferred_element_type=F32)
    qi = jnp.dot(cb, w_iq_ref[...], preferred_element_type=F32)
    qi_rot = jnp.dot(cb, w_iqr_ref[...], preferred_element_type=F32)
    for j in range(D_MODEL // LANES):
        sl = slice(j * LANES, (j + 1) * LANES)
        q_ref[:, sl] = (_rope_mix(q[:, sl], q_rot[:, sl], cq, sq) * scale).astype(BF16)
        qi_ref[:, sl] = _rope_mix(qi[:, sl], qi_rot[:, sl], ci, si).astype(BF16)
    o = Q_LORA_RANK
    k_ref[...] = _rope_mix(h[:, o:o + LANES], h[:, o + 4 * LANES:o + 5 * LANES], cq, sq).astype(BF16)
    vt = h[:, o + LANES:o + 2 * LANES].T
    tm = vt.shape[1]
    pad = VT_ROWS - ATT_HEAD_DIM
    ones_row = jnp.where(lax.broadcasted_iota(jnp.int32, (pad, tm), 0) == 0, 1.0, 0.0)
    for g in range(ATT_KV_HEADS):
        vt_ref[g * VT_ROWS:g * VT_ROWS + ATT_HEAD_DIM, :] = vt[g * ATT_HEAD_DIM:(g + 1) * ATT_HEAD_DIM].astype(BF16)
        vt_ref[g * VT_ROWS + ATT_HEAD_DIM:(g + 1) * VT_ROWS, :] = ones_row.astype(BF16)
    ki = _layer_norm(h[:, o + 2 * LANES:o + 3 * LANES], g_ik_ref[...], b_ik_ref[...])
    ki_ref[...] = _rope(ki, ci, si, IDX_ROT_DIM // 2, IDX_DIM).astype(BF16)
    wi = h[:, o + 3 * LANES:o + 4 * LANES] * (IDX_HEADS ** -0.5 * IDX_DIM ** -0.5)
    wit_ref[...] = wi.T[:IDX_HEADS]


def _dsa_proj(x2, w_in_p, g_cq, w_uq, w_uqr, w_iq, w_iqr, g_ik, b_ik, cq, sq, ci, si, tm):
    n = x2.shape[0]
    row = lambda w: pl.BlockSpec((tm, w), lambda i: (i, 0))
    col = lambda r: pl.BlockSpec((r, tm), lambda i: (0, i))
    return pl.pallas_call(
        _dsa_proj_kernel,
        grid=(n // tm,),
        in_specs=[row(D_MODEL), _const_spec(w_in_p.shape), _const_spec(g_cq.shape),
                  _const_spec(w_uq.shape), _const_spec(w_uqr.shape), _const_spec(w_iq.shape),
                  _const_spec(w_iqr.shape), _const_spec(g_ik.shape), _const_spec(b_ik.shape),
                  row(LANES), row(LANES), row(LANES), row(LANES)],
        out_specs=[row(D_MODEL), row(LANES), col(ATT_KV_HEADS * VT_ROWS), row(D_MODEL), row(LANES),
                   col(IDX_HEADS)],
        out_shape=[jax.ShapeDtypeStruct((n, D_MODEL), BF16), jax.ShapeDtypeStruct((n, LANES), BF16),
                   jax.ShapeDtypeStruct((ATT_KV_HEADS * VT_ROWS, n), BF16),
                   jax.ShapeDtypeStruct((n, D_MODEL), BF16),
                   jax.ShapeDtypeStruct((n, LANES), BF16), jax.ShapeDtypeStruct((IDX_HEADS, n), F32)],
        compiler_params=pltpu.CompilerParams(dimension_semantics=("arbitrary",),
                                             vmem_limit_bytes=VMEM_LIMIT),
        name="dsa_proj",
    )(x2, w_in_p, g_cq, w_uq, w_uqr, w_iq, w_iqr, g_ik, b_ik, cq, sq, ci, si)


VT_ROWS = 80
BOUND_SLACK = 1.01
MIN_DENOM = 2.0 ** -60
LOOKAHEAD = 8
INT_MIN = -2 ** 31
KEY_NEG_INF = INT_MIN + 0x7FFFFF


def _skey_to_float(key, high_half=False):
    key = jnp.maximum(key, jnp.int32(KEY_NEG_INF))
    bits = key ^ ((key >> 31) & jnp.int32(0x7FFFFFFF))
    if high_half:
        bits = bits & jnp.int32(-65536)
    return lax.bitcast_convert_type(bits, F32)


def _dsa_attn_kernel(q_ref, qi_ref, wit_ref, k_ref, vt_ref, ki_ref, o_ref, s_ref, s16_ref, m_ref, acc_ref, ot_ref,
                     kmax_ref,
                     *, qt, topk, seq):
    i = pl.program_id(1)
    nch = i + 1
    kc = qt
    krow = lax.broadcasted_iota(jnp.int32, (kc, qt), 0)
    qcol = lax.broadcasted_iota(jnp.int32, (kc, qt), 1)

    def chunk_rows(c):
        return pl.ds(pl.multiple_of(c * kc, kc), kc)

    wit = wit_ref[...]

    def score_chunk(c, carry):
        kblk = ki_ref[chunk_rows(c), :]
        acc = None
        for h in range(IDX_HEADS):
            r = lax.dot_general(kblk, qi_ref[:, h * IDX_DIM:(h + 1) * IDX_DIM], NT_DIMS,
                                preferred_element_type=F32)
            contrib = jnp.maximum(r, 0.0) * wit[h:h + 1, :]
            acc = contrib if acc is None else acc + contrib
        s_ref[c] = acc
        s16_ref[c] = acc.astype(BF16)
        return carry

    lax.fori_loop(0, nch, score_chunk, 0)
    diag = jnp.where(krow <= qcol, s_ref[i], -jnp.inf)
    s_ref[i] = diag
    s16_ref[i] = diag.astype(BF16)

    def count(pred):
        def body(c, acc):
            m = pred(s_ref[c], c)
            ones = [jnp.where(m[r * SUBLANES:(r + 1) * SUBLANES], 1.0, 0.0) for r in range(kc // SUBLANES)]
            return acc + _tree(jnp.add, ones)
        acc = lax.fori_loop(0, nch, body, jnp.zeros((SUBLANES, qt), F32))
        return jnp.sum(acc, axis=0, keepdims=True)

    def count_rounded(cf16):
        rows16 = 2 * SUBLANES
        one, zero = jnp.ones((rows16, qt), BF16), jnp.zeros((rows16, qt), BF16)

        def body(c, acc):
            m = s16_ref[c] >= cf16
            ones = [jnp.where(m[r * rows16:(r + 1) * rows16], one, zero) for r in range(kc // rows16)]
            return acc + _tree(jnp.add, ones).astype(F32)
        acc = lax.fori_loop(0, nch, body, jnp.zeros((rows16, qt), F32))
        return jnp.sum(acc, axis=0, keepdims=True)

    def coarse_bit(it, ans):
        cand = ans | lax.shift_left(jnp.int32(1), 31 - it)
        b16 = _skey_to_float(cand ^ jnp.int32(INT_MIN), high_half=True).astype(BF16)
        return jnp.where(count_rounded(b16) >= topk, cand, ans)

    coarse = lax.fori_loop(0, 16, coarse_bit, jnp.zeros((1, qt), jnp.int32)) ^ jnp.int32(INT_MIN)
    coarse = jnp.maximum(coarse, jnp.int32(KEY_NEG_INF))
    key_b = jnp.where(coarse < 0, coarse | jnp.int32(0xFFFF), coarse)
    base = key_b - jnp.int32(2 ** 15 + 1)

    def fine_bit(it, carry):
        off, cnt_ans = carry
        cand = off | lax.shift_left(jnp.int32(1), 16 - it)
        cf = _skey_to_float(base + cand)
        cnt = count(lambda blk, c: blk >= cf)
        keep = cnt >= topk
        return jnp.where(keep, cand, off), jnp.where(keep, cnt, cnt_ans)

    base_f = _skey_to_float(base)
    off, cnt_ge = lax.fori_loop(0, 17, fine_bit,
                                (jnp.zeros((1, qt), jnp.int32), count(lambda blk, c: blk >= base_f)))
    tau = _skey_to_float(base + off)
    cnt_gt = count(lambda blk, c: blk > tau)
    need = topk - cnt_gt
    finite = tau > -jnp.inf
    excess = jnp.where((cnt_ge - cnt_gt > need) & finite, 1.0, 0.0)

    def index_cut():
        nbits = (seq - 1).bit_length()

        def index_bit(it, p):
            cand = p | lax.shift_left(jnp.int32(1), nbits - 1 - it)
            cnt = count(lambda blk, c: (blk == tau) & (krow + c * kc < cand))
            return jnp.where(cnt < need, cand, p)
        return lax.fori_loop(0, nbits, index_bit, jnp.zeros((1, qt), jnp.int32))

    cut = lax.cond(jnp.max(excess) > 0.0, index_cut, lambda: jnp.full((1, qt), seq, jnp.int32))
    cut = jnp.where(finite, cut, -1)

    def write_bias(c, carry):
        blk = s_ref[c]
        sel = (blk > tau) | ((blk == tau) & (krow + c * kc <= cut))
        s_ref[c] = jnp.where(sel, 0.0, MASKED)
        return carry

    lax.fori_loop(0, nch, write_bias, 0)

    group = ATT_HEADS // ATT_KV_HEADS

    def chunk_operands(c):
        kgs = [k_ref[chunk_rows(c), g * ATT_HEAD_DIM:(g + 1) * ATT_HEAD_DIM] for g in range(ATT_KV_HEADS)]
        vtgs = [vt_ref[g * VT_ROWS:(g + 1) * VT_ROWS, chunk_rows(c)] for g in range(ATT_KV_HEADS)]
        bias = s_ref[c]

        def logits(h):
            return lax.dot_general(kgs[h // group], q_ref[:, h * ATT_HEAD_DIM:(h + 1) * ATT_HEAD_DIM],
                                   NT_DIMS, preferred_element_type=F32) + bias
        return logits, vtgs

    def pipelined(logits, consume):
        sts = {h: logits(h) for h in range(LOOKAHEAD)}
        for h in range(ATT_HEADS):
            p = consume(h, sts.pop(h))
            if h + LOOKAHEAD < ATT_HEADS:
                sts[h + LOOKAHEAD] = logits(h + LOOKAHEAD)
            yield h, p

    @pl.when(i == 0)
    def _():
        kf = k_ref[...].astype(F32)
        sq = kf * kf
        for g in range(ATT_KV_HEADS):
            n2 = jnp.sum(sq[:, g * ATT_HEAD_DIM:(g + 1) * ATT_HEAD_DIM], axis=1, keepdims=True)
            kmax_ref[g] = jnp.max(n2)

    qf = q_ref[...].astype(F32)
    head_of_lane = lax.broadcasted_iota(jnp.int32, (ATT_HEADS, D_MODEL), 1) // ATT_HEAD_DIM
    head_sel = jnp.where(head_of_lane == lax.broadcasted_iota(jnp.int32, (ATT_HEADS, D_MODEL), 0), 1.0, 0.0)
    qn2 = lax.dot_general(head_sel.astype(BF16), (qf * qf).astype(BF16), NT_DIMS,
                          preferred_element_type=F32)
    for h in range(ATT_HEADS):
        m_ref[h] = jnp.sqrt(qn2[h:h + 1, :] * kmax_ref[h // group]) * BOUND_SLACK

    acc_ref[...] = jnp.zeros(acc_ref.shape, F32)

    def attend_fast(c, carry):
        logits, vtgs = chunk_operands(c)
        for h, p in pipelined(logits, lambda h, st: jnp.exp2(st - m_ref[h]).astype(BF16)):
            acc_ref[h] = acc_ref[h] + jnp.dot(vtgs[h // group], p, preferred_element_type=F32)
        return carry

    lax.fori_loop(0, nch, attend_fast, 0)
    denom = _tree(jnp.minimum, [acc_ref[h, ATT_HEAD_DIM:ATT_HEAD_DIM + 1, :] for h in range(ATT_HEADS)])

    @pl.when(jnp.logical_not(jnp.min(denom) >= MIN_DENOM))
    def _():
        m_ref[...] = jnp.full(m_ref.shape, MASKED, F32)
        acc_ref[...] = jnp.zeros(acc_ref.shape, F32)

        def attend(c, carry):
            logits, vtgs = chunk_operands(c)
            alphas = {}

            def softmax_step(h, st):
                mx = _tree(jnp.maximum, [st[r * SUBLANES:(r + 1) * SUBLANES] for r in range(kc // SUBLANES)])
                m_old = m_ref[h]
                m_new = jnp.maximum(m_old, jnp.max(mx, axis=0, keepdims=True))
                m_ref[h] = m_new
                alphas[h] = jnp.exp2(m_old - m_new)
                return jnp.exp2(st - m_new).astype(BF16)

            for h, p in pipelined(logits, softmax_step):
                acc_ref[h] = alphas[h] * acc_ref[h] + jnp.dot(vtgs[h // group], p, preferred_element_type=F32)
            return carry

        lax.fori_loop(0, nch, attend, 0)

    for h in range(ATT_HEADS):
        acc = acc_ref[h]
        ot_ref[h * ATT_HEAD_DIM:(h + 1) * ATT_HEAD_DIM, :] = (
            acc[:ATT_HEAD_DIM] * (1.0 / acc[ATT_HEAD_DIM:ATT_HEAD_DIM + 1]))
    o_ref[...] = ot_ref[...].T.astype(BF16)


def _dsa_attn(q, qi, wit, k, vt, ki, batch, seq, qt):
    nq = seq // qt
    topk = min(INDEX_TOPK, seq // 4)
    qrow = lambda w: pl.BlockSpec((qt, w), lambda b, i: (b * nq + i, 0))
    kv = pl.BlockSpec((seq, LANES), lambda b, i: (b, 0))
    return pl.pallas_call(
        functools.partial(_dsa_attn_kernel, qt=qt, topk=topk, seq=seq),
        grid=(batch, nq),
        in_specs=[qrow(D_MODEL), qrow(D_MODEL),
                  pl.BlockSpec((IDX_HEADS, qt), lambda b, i: (0, b * nq + i)),
                  kv, pl.BlockSpec((ATT_KV_HEADS * VT_ROWS, seq), lambda b, i: (0, b)), kv],
        out_specs=qrow(D_MODEL),
        out_shape=jax.ShapeDtypeStruct((batch * seq, D_MODEL), BF16),
        scratch_shapes=[pltpu.VMEM((nq, qt, qt), F32),
                        pltpu.VMEM((nq, qt, qt), BF16),
                        pltpu.VMEM((ATT_HEADS, 1, qt), F32),
                        pltpu.VMEM((ATT_HEADS, VT_ROWS, qt), F32),
                        pltpu.VMEM((D_MODEL, qt), F32),
                        pltpu.SMEM((ATT_KV_HEADS,), F32)],
        compiler_params=pltpu.CompilerParams(dimension_semantics=("arbitrary", "arbitrary"),
                                             vmem_limit_bytes=VMEM_LIMIT),
        name="dsa_attn",
    )(q, qi, wit, k, vt, ki)


FF_CHUNKS = ((0, 768), (768, 768), (1536, 768), (2304, 512))


def _post_kernel(a_ref, x_ref, wo_ref, g1_ref, b1_ref, wg_ref, wu_ref, wd_ref, g2_ref, b2_ref, o_ref):
    y = jnp.dot(a_ref[...], wo_ref[...], preferred_element_type=F32)
    h = _layer_norm(DEEPNORM_ALPHA * x_ref[...] + y, g1_ref[...], b1_ref[...])
    hb = h.astype(BF16)
    f = jnp.zeros(h.shape, F32)
    for start, size in FF_CHUNKS:
        gate = jnp.dot(hb, wg_ref[:, start:start + size], preferred_element_type=F32)
        up = jnp.dot(hb, wu_ref[:, start:start + size], preferred_element_type=F32)
        act = (gate * _sigmoid(gate) * up).astype(BF16)
        f = f + jnp.dot(act, wd_ref[start:start + size, :], preferred_element_type=F32)
    o_ref[...] = _layer_norm(DEEPNORM_ALPHA * h + f, g2_ref[...], b2_ref[...])


def _post(a, x2, wo, g1, b1, wg, wu, wd, g2, b2, tm):
    n = x2.shape[0]
    row = pl.BlockSpec((tm, D_MODEL), lambda i: (i, 0))
    return pl.pallas_call(
        _post_kernel,
        grid=(n // tm,),
        in_specs=[row, row] + [_const_spec(t.shape) for t in (wo, g1, b1, wg, wu, wd, g2, b2)],
        out_specs=row,
        out_shape=jax.ShapeDtypeStruct((n, D_MODEL), F32),
        compiler_params=pltpu.CompilerParams(dimension_semantics=("arbitrary",),
                                             vmem_limit_bytes=VMEM_LIMIT),
        name="post_ffn",
    )(a, x2, wo, g1, b1, wg, wu, wd, g2, b2)


CHUNK = 64
SUB = 16


def _split3(x):
    hi = x.astype(BF16)
    r1 = x - hi.astype(F32)
    mid = r1.astype(BF16)
    lo = (r1 - mid.astype(F32)).astype(BF16)
    return hi, mid, lo


def _hgrn_chunk(qs, ks, vs, lfs, sts, b_s, k_s, v_s):
    heads = range(len(qs))
    rowi = lax.broadcasted_iota(jnp.int32, (CHUNK, CHUNK), 0)
    coli = lax.broadcasted_iota(jnp.int32, (CHUNK, CHUNK), 1)
    tri = jnp.where(rowi >= coli, 1.0, 0.0).astype(BF16)
    nsub = CHUNK // SUB

    bs = []
    for h in heads:
        hi, mid, lo = _split3(lfs[h])
        b = (jnp.dot(tri, hi, preferred_element_type=F32) + jnp.dot(tri, mid, preferred_element_type=F32)
             + jnp.dot(tri, lo, preferred_element_type=F32))
        bs.append(b * LOG2E)
    for h in heads:
        b_s[h] = bs[h]
        k_s[h] = ks[h]
        v_s[h] = vs[h]

    os_ = [lax.dot_general((qs[h] * jnp.exp2(bs[h])).astype(BF16), sts[h].astype(BF16), NT_DIMS,
                           preferred_element_type=F32) for h in heads]

    for j in range(nsub - 1):
        r0, r1 = j * SUB, (j + 1) * SUB
        attn = []
        for h in heads:
            e_j = b_s[h, r1 - 1:r1, :]
            kt = (ks[h][r0:r1] * jnp.exp2(e_j - bs[h][r0:r1])).astype(BF16)
            qt = (qs[h][r1:] * jnp.exp2(bs[h][r1:] - e_j)).astype(BF16)
            attn.append(lax.dot_general(qt, kt, NT_DIMS, preferred_element_type=F32))
        for h in heads:
            upd = jnp.dot(attn[h].astype(BF16), vs[h][r0:r1].astype(BF16), preferred_element_type=F32)
            os_[h] = jnp.concatenate([os_[h][:r1], os_[h][r1:] + upd], axis=0)

    new_sts = []
    for h in heads:
        b_last = b_s[h, CHUNK - 1:CHUNK, :]
        kh = (ks[h] * jnp.exp2(b_last - bs[h])).astype(BF16)
        new_sts.append(sts[h] * jnp.exp2(b_last)
                       + lax.dot_general(vs[h].astype(BF16), kh, TN_DIMS, preferred_element_type=F32))

    for h in heads:
        b, q = bs[h], qs[h]
        pieces = []
        for j in range(nsub):
            r0 = j * SUB
            for half in range(SUB // SUBLANES):
                t0 = r0 + half * SUBLANES
                bq = b[t0:r0 + SUB]
                qq = q[t0:r0 + SUB]
                trow = lax.broadcasted_iota(jnp.int32, bq.shape, 0)
                acc = jnp.zeros(bq.shape, F32)
                for sl in range(SUBLANES):
                    s = t0 + sl
                    d = jnp.where(trow >= sl, bq - b_s[h, s:s + 1, :], -jnp.inf)
                    p = jnp.exp2(d) * (qq * k_s[h, s:s + 1, :])
                    acc = acc + jnp.sum(p, axis=1, keepdims=True) * v_s[h, s:s + 1, :]
                if half == 0:
                    first = acc
                else:
                    pieces.append(first[:half * SUBLANES])
                    pieces.append(first[half * SUBLANES:] + acc)
        os_[h] = os_[h] + jnp.concatenate(pieces, axis=0)
    return os_, new_sts


def _hgrn_kernel(x_ref, w_ref, lb_ref, gn_ref, a_ref, st_ref, q_s, k_s, v_s, lf_s, o_s, bc_s, kc_s, vc_s, *, tc):
    @pl.when(pl.program_id(1) == 0)
    def _():
        st_ref[...] = jnp.zeros(st_ref.shape, F32)

    d = D_MODEL
    xb = x_ref[...].astype(BF16)
    lb = lb_ref[...]
    hq = jnp.dot(xb, w_ref[:, 0:d], preferred_element_type=F32)
    q = hq * _sigmoid(hq)
    hf = jnp.dot(xb, w_ref[:, d:2 * d], preferred_element_type=F32)
    fg = lb + (1.0 - lb) * _sigmoid(hf)
    k = 1.0 - fg
    lf = jnp.log(fg)
    v = jnp.dot(xb, w_ref[:, 2 * d:3 * d], preferred_element_type=F32)
    for h in range(HGRN_HEADS):
        sl = slice(h * HGRN_EXPAND, (h + 1) * HGRN_EXPAND)
        q_s[h] = q[:, sl]
        k_s[h] = k[:, sl]
        v_s[h] = v[:, sl]
        lf_s[h] = lf[:, sl]

    heads = range(HGRN_HEADS)

    def chunk_body(ci, carry):
        rows = pl.ds(pl.multiple_of(ci * CHUNK, CHUNK), CHUNK)
        outs, sts = _hgrn_chunk([q_s[h, rows, :] for h in heads], [k_s[h, rows, :] for h in heads],
                                [v_s[h, rows, :] for h in heads], [lf_s[h, rows, :] for h in heads],
                                [st_ref[h] for h in heads], bc_s, kc_s, vc_s)
        for h in heads:
            o_s[h, rows, :] = outs[h]
            st_ref[h] = sts[h]
        return carry

    lax.fori_loop(0, tc // CHUNK, chunk_body, 0)

    hg = jnp.dot(xb, w_ref[:, 3 * d:4 * d], preferred_element_type=F32)
    gate = hg * _sigmoid(hg)
    gn = gn_ref[...]
    for h in range(HGRN_HEADS):
        sl = slice(h * HGRN_EXPAND, (h + 1) * HGRN_EXPAND)
        o = o_s[h]
        ms = jnp.mean(o * o, axis=-1, keepdims=True)
        a_ref[:, sl] = (o * lax.rsqrt(ms + RMS_EPS) * gn * gate[:, sl]).astype(BF16)


def _hgrn(x2, w_in, lb, g_norm, batch, seq, tc):
    nt = seq // tc
    row = pl.BlockSpec((tc, D_MODEL), lambda b, t: (b * nt + t, 0))
    hm = lambda: pltpu.VMEM((HGRN_HEADS, tc, HGRN_EXPAND), F32)
    cm = lambda: pltpu.VMEM((HGRN_HEADS, CHUNK, HGRN_EXPAND), F32)
    return pl.pallas_call(
        functools.partial(_hgrn_kernel, tc=tc),
        grid=(batch, nt),
        in_specs=[row, _const_spec(w_in.shape), _const_spec(lb.shape), _const_spec(g_norm.shape)],
        out_specs=row,
        out_shape=jax.ShapeDtypeStruct((batch * seq, D_MODEL), BF16),
        scratch_shapes=[pltpu.VMEM((HGRN_HEADS, HGRN_EXPAND, HGRN_EXPAND), F32),
                        hm(), hm(), hm(), hm(), hm(), cm(), cm(), cm()],
        compiler_params=pltpu.CompilerParams(dimension_semantics=("arbitrary", "arbitrary"),
                                             vmem_limit_bytes=VMEM_LIMIT),
        name="hgrn2",
    )(x2, w_in, lb, g_norm)


def _rope_tables(positions, rot_dim, period):
    half = rot_dim // 2
    inv = ROPE_THETA ** (-jnp.arange(0, rot_dim, 2, dtype=F32) / rot_dim)
    ang = positions.astype(F32)[..., None] * inv
    cos, sin = jnp.cos(ang), jnp.sin(ang)
    rest = ang.shape[:-1] + (period - 2 * half,)
    c = jnp.concatenate([cos, cos, jnp.ones(rest, F32)], axis=-1)
    s = jnp.concatenate([-sin, sin, jnp.zeros(rest, F32)], axis=-1)
    reps = LANES // period
    c = jnp.tile(c, (1, 1, reps)).reshape(-1, LANES)
    s = jnp.tile(s, (1, 1, reps)).reshape(-1, LANES)
    return c, s


def _row(v):
    return v.reshape(1, -1).astype(F32)


def _rotate_half_columns(w, half, period):
    j = np.arange(w.shape[1])
    r = j % period
    src_col = np.where(r < half, j + half, np.where(r < 2 * half, j - half, j))
    return w[:, src_col]


def _dsa_layer(x2, positions, w_in, g_cq, w_uq, w_iq, g_ik, b_ik, batch, seq, tm, qt):
    cq, sq = _rope_tables(positions, ATT_ROT_DIM, ATT_HEAD_DIM)
    ci, si = _rope_tables(positions, IDX_ROT_DIM, IDX_DIM)
    w_k = w_in[:, Q_LORA_RANK:Q_LORA_RANK + ATT_KV_DIM]
    w_in_p = jnp.concatenate([jnp.pad(w_in, ((0, 0), (0, ATT_IN_PAD - w_in.shape[1]))),
                              _rotate_half_columns(w_k, ATT_ROT_DIM // 2, ATT_HEAD_DIM)], axis=1).astype(BF16)
    q, k, vt, qi, ki, wit = _dsa_proj(
        x2, w_in_p, _row(g_cq), w_uq.astype(BF16),
        _rotate_half_columns(w_uq, ATT_ROT_DIM // 2, ATT_HEAD_DIM).astype(BF16), w_iq.astype(BF16),
        _rotate_half_columns(w_iq, IDX_ROT_DIM // 2, IDX_DIM).astype(BF16),
        _row(g_ik), _row(b_ik), cq, sq, ci, si, tm)
    return _dsa_attn(q, qi, wit, k, vt, ki, batch, seq, qt)


def kernel(x, positions, att_w_in, att_g_cq, att_w_uq, att_w_iq, att_g_ik, att_b_ik, att_w_o,
           hgrn_lb_logits, hgrn_w_in, hgrn_g_norm, hgrn_w_o,
           ffn_w_gate, ffn_w_up, ffn_w_down, ln_g, ln_b):
    batch, seq, d = x.shape
    n = batch * seq
    tm = min(512, n)
    qt = min(256, seq)
    tc = min(256, seq)
    lb_all = jnp.cumsum(jax.nn.softmax(hgrn_lb_logits.astype(F32), axis=0), axis=0)
    lb_all = lb_all - lb_all[0]

    h = x.reshape(n, d)
    for layer in range(DEPTH):
        j = layer // 2
        if layer % 2 == 0:
            a = _dsa_layer(h, positions, att_w_in[j], att_g_cq[j], att_w_uq[j], att_w_iq[j],
                           att_g_ik[j], att_b_ik[j], batch, seq, tm, qt)
            w_o = att_w_o[j]
        else:
            a = _hgrn(h, hgrn_w_in[j].astype(BF16), _row(lb_all[layer]), _row(hgrn_g_norm[j]),
                      batch, seq, tc)
            w_o = hgrn_w_o[j]
        h = _post(a, h, w_o.astype(BF16), _row(ln_g[layer, 0]), _row(ln_b[layer, 0]),
                  ffn_w_gate[layer].astype(BF16), ffn_w_up[layer].astype(BF16),
                  ffn_w_down[layer].astype(BF16), _row(ln_g[layer, 1]), _row(ln_b[layer, 1]), tm)
    return h.reshape(batch, seq, d)
```

```python
import functools

import jax
import numpy as np
import jax.numpy as jnp
from jax import lax
from jax.experimental import pallas as pl
from jax.experimental.pallas import tpu as pltpu

F32 = jnp.float32
BF16 = jnp.bfloat16

D_MODEL = 1024
DEPTH = 2
ATT_HEADS = 16
ATT_HEAD_DIM = 64
ATT_KV_HEADS = 2
Q_LORA_RANK = 256
IDX_HEADS = 8
IDX_DIM = 128
INDEX_TOPK = 256
ATT_KV_DIM = ATT_KV_HEADS * ATT_HEAD_DIM
HGRN_EXPAND = 128
HGRN_HEADS = D_MODEL // HGRN_EXPAND
D_FF = 2816
ROPE_THETA = 500000.0
ATT_ROT_DIM = ATT_HEAD_DIM // 4
IDX_ROT_DIM = IDX_DIM // 4
DEEPNORM_ALPHA = (2 * DEPTH) ** 0.25
LN_EPS = 1e-5
RMS_EPS = 1e-6

LANES = 128
SUBLANES = 8
VMEM_LIMIT = 56 * 1024 * 1024
ATT_IN_PAD = 768
MASKED = -1e30
LOG2E = 1.4426950408889634
NT_DIMS = (((1,), (1,)), ((), ()))
TN_DIMS = (((0,), (0,)), ((), ()))


def _const_spec(shape):
    nd = len(shape)
    return pl.BlockSpec(shape, lambda *_: (0,) * nd, pipeline_mode=pl.Buffered(1))


def _layer_norm(x, g, b):
    mu = jnp.mean(x, axis=-1, keepdims=True)
    xc = x - mu
    var = jnp.mean(xc * xc, axis=-1, keepdims=True)
    return xc * lax.rsqrt(var + LN_EPS) * g + b


def _sigmoid(x):
    return 1.0 / (1.0 + jnp.exp(-x))


def _tree(op, xs):
    while len(xs) > 1:
        xs = [op(xs[i], xs[i + 1]) for i in range(0, len(xs) - 1, 2)] + ([xs[-1]] if len(xs) % 2 else [])
    return xs[0]


def _rope(t, c, s, half, period):
    lane = lax.broadcasted_iota(jnp.int32, t.shape, 1) % period
    up = pltpu.roll(t, LANES - half, 1)
    dn = pltpu.roll(t, half, 1)
    return t * c + jnp.where(lane < half, up, dn) * s


def _rope_mix(t, t_rot, c, s):
    return t * c + t_rot * s


def _dsa_proj_kernel(x_ref, w_in_ref, g_cq_ref, w_uq_ref, w_uqr_ref, w_iq_ref, w_iqr_ref, g_ik_ref, b_ik_ref,
                     cq_ref, sq_ref, ci_ref, si_ref,
                     q_ref, k_ref, vt_ref, qi_ref, ki_ref, wit_ref):
    xb = x_ref[...].astype(BF16)
    h = jnp.dot(xb, w_in_ref[...], preferred_element_type=F32)
    c = h[:, :Q_LORA_RANK]
    ms = jnp.mean(c * c, axis=-1, keepdims=True)
    cb = (c * lax.rsqrt(ms + RMS_EPS) * g_cq_ref[...]).astype(BF16)
    cq, sq, ci, si = cq_ref[...], sq_ref[...], ci_ref[...], si_ref[...]
    scale = ATT_HEAD_DIM ** -0.5 * LOG2E
    q = jnp.dot(cb, w_uq_ref[...], preferred_element_type=F32)
    q_rot = jnp.dot(cb, w_uqr_ref[...], preferred_element_type=F32)
    qi = jnp.dot(cb, w_iq_ref[...], preferred_element_type=F32)
    qi_rot = jnp.dot(cb, w_iqr_ref[...], preferred_element_type=F32)
    for j in range(D_MODEL // LANES):
        sl = slice(j * LANES, (j + 1) * LANES)
        q_ref[:, sl] = (_rope_mix(q[:, sl], q_rot[:, sl], cq, sq) * scale).astype(BF16)
        qi_ref[:, sl] = _rope_mix(qi[:, sl], qi_rot[:, sl], ci, si).astype(BF16)
    o = Q_LORA_RANK
    k_ref[...] = _rope_mix(h[:, o:o + LANES], h[:, o + 4 * LANES:o + 5 * LANES], cq, sq).astype(BF16)
    vt = h[:, o + LANES:o + 2 * LANES].T
    tm = vt.shape[1]
    pad = VT_ROWS - ATT_HEAD_DIM
    ones_row = jnp.where(lax.broadcasted_iota(jnp.int32, (pad, tm), 0) == 0, 1.0, 0.0)
    for g in range(ATT_KV_HEADS):
        vt_ref[g * VT_ROWS:g * VT_ROWS + ATT_HEAD_DIM, :] = vt[g * ATT_HEAD_DIM:(g + 1) * ATT_HEAD_DIM].astype(BF16)
        vt_ref[g * VT_ROWS + ATT_HEAD_DIM:(g + 1) * VT_ROWS, :] = ones_row.astype(BF16)
    ki = _layer_norm(h[:, o + 2 * LANES:o + 3 * LANES], g_ik_ref[...], b_ik_ref[...])
    ki_ref[...] = _rope(ki, ci, si, IDX_ROT_DIM // 2, IDX_DIM).astype(BF16)
    wi = h[:, o + 3 * LANES:o + 4 * LANES] * (IDX_HEADS ** -0.5 * IDX_DIM ** -0.5)
    wit_ref[...] = wi.T[:IDX_HEADS]


def _dsa_proj(x2, w_in_p, g_cq, w_uq, w_uqr, w_iq, w_iqr, g_ik, b_ik, cq, sq, ci, si, tm):
    n = x2.shape[0]
    row = lambda w: pl.BlockSpec((tm, w), lambda i: (i, 0))
    col = lambda r: pl.BlockSpec((r, tm), lambda i: (0, i))
    return pl.pallas_call(
        _dsa_proj_kernel,
        grid=(n // tm,),
        in_specs=[row(D_MODEL), _const_spec(w_in_p.shape), _const_spec(g_cq.shape),
                  _const_spec(w_uq.shape), _const_spec(w_uqr.shape), _const_spec(w_iq.shape),
                  _const_spec(w_iqr.shape), _const_spec(g_ik.shape), _const_spec(b_ik.shape),
                  row(LANES), row(LANES), row(LANES), row(LANES)],
        out_specs=[row(D_MODEL), row(LANES), col(ATT_KV_HEADS * VT_ROWS), row(D_MODEL), row(LANES),
                   col(IDX_HEADS)],
        out_shape=[jax.ShapeDtypeStruct((n, D_MODEL), BF16), jax.ShapeDtypeStruct((n, LANES), BF16),
                   jax.ShapeDtypeStruct((ATT_KV_HEADS * VT_ROWS, n), BF16),
                   jax.ShapeDtypeStruct((n, D_MODEL), BF16),
                   jax.ShapeDtypeStruct((n, LANES), BF16), jax.ShapeDtypeStruct((IDX_HEADS, n), F32)],
        compiler_params=pltpu.CompilerParams(dimension_semantics=("arbitrary",),
                                             vmem_limit_bytes=VMEM_LIMIT),
        name="dsa_proj",
    )(x2, w_in_p, g_cq, w_uq, w_uqr, w_iq, w_iqr, g_ik, b_ik, cq, sq, ci, si)


VT_ROWS = 80
BOUND_SLACK = 1.01
MIN_DENOM = 2.0 ** -60
LOOKAHEAD = 8
INT_MIN = -2 ** 31
KEY_NEG_INF = INT_MIN + 0x7FFFFF


def _skey_to_float(key, high_half=False):
    key = jnp.maximum(key, jnp.int32(KEY_NEG_INF))
    bits = key ^ ((key >> 31) & jnp.int32(0x7FFFFFFF))
    if high_half:
        bits = bits & jnp.int32(-65536)
    return lax.bitcast_convert_type(bits, F32)


def _dsa_attn_kernel(q_ref, qi_ref, wit_ref, k_ref, vt_ref, ki_ref, o_ref, s_ref, s16_ref, m_ref, acc_ref, ot_ref,
                     kmax_ref,
                     *, qt, topk, seq):
    i = pl.program_id(1)
    nch = i + 1
    kc = qt
    krow = lax.broadcasted_iota(jnp.int32, (kc, qt), 0)
    qcol = lax.broadcasted_iota(jnp.int32, (kc, qt), 1)

    def chunk_rows(c):
        return pl.ds(pl.multiple_of(c * kc, kc), kc)

    wit = wit_ref[...]

    def score_chunk(c, carry):
        kblk = ki_ref[chunk_rows(c), :]
        acc = None
        for h in range(IDX_HEADS):
            r = lax.dot_general(kblk, qi_ref[:, h * IDX_DIM:(h + 1) * IDX_DIM], NT_DIMS,
                                preferred_element_type=F32)
            contrib = jnp.maximum(r, 0.0) * wit[h:h + 1, :]
            acc = contrib if acc is None else acc + contrib
        s_ref[c] = acc
        s16_ref[c] = acc.astype(BF16)
        return carry

    lax.fori_loop(0, nch, score_chunk, 0)
    diag = jnp.where(krow <= qcol, s_ref[i], -jnp.inf)
    s_ref[i] = diag
    s16_ref[i] = diag.astype(BF16)

    def count(pred):
        def body(c, acc):
            m = pred(s_ref[c], c)
            ones = [jnp.where(m[r * SUBLANES:(r + 1) * SUBLANES], 1.0, 0.0) for r in range(kc // SUBLANES)]
            return acc + _tree(jnp.add, ones)
        acc = lax.fori_loop(0, nch, body, jnp.zeros((SUBLANES, qt), F32))
        return jnp.sum(acc, axis=0, keepdims=True)

    def count_rounded(cf16):
        rows16 = 2 * SUBLANES
        one, zero = jnp.ones((rows16, qt), BF16), jnp.zeros((rows16, qt), BF16)

        def body(c, acc):
            m = s16_ref[c] >= cf16
            ones = [jnp.where(m[r * rows16:(r + 1) * rows16], one, zero) for r in range(kc // rows16)]
            return acc + _tree(jnp.add, ones).astype(F32)
        acc = lax.fori_loop(0, nch, body, jnp.zeros((rows16, qt), F32))
        return jnp.sum(acc, axis=0, keepdims=True)

    def coarse_bit(it, ans):
        cand = ans | lax.shift_left(jnp.int32(1), 31 - it)
        b16 = _skey_to_float(cand ^ jnp.int32(INT_MIN), high_half=True).astype(BF16)
        return jnp.where(count_rounded(b16) >= topk, cand, ans)

    coarse = lax.fori_loop(0, 16, coarse_bit, jnp.zeros((1, qt), jnp.int32)) ^ jnp.int32(INT_MIN)
    coarse = jnp.maximum(coarse, jnp.int32(KEY_NEG_INF))
    key_b = jnp.where(coarse < 0, coarse | jnp.int32(0xFFFF), coarse)
    base = key_b - jnp.int32(2 ** 15 + 1)

    def fine_bit(it, carry):
        off, cnt_ans = carry
        cand = off | lax.shift_left(jnp.int32(1), 16 - it)
        cf = _skey_to_float(base + cand)
        cnt = count(lambda blk, c: blk >= cf)
        keep = cnt >= topk
        return jnp.where(keep, cand, off), jnp.where(keep, cnt, cnt_ans)

    base_f = _skey_to_float(base)
    off, cnt_ge = lax.fori_loop(0, 17, fine_bit,
                                (jnp.zeros((1, qt), jnp.int32), count(lambda blk, c: blk >= base_f)))
    tau = _skey_to_float(base + off)
    finite = tau > -jnp.inf
    excess = jnp.where((cnt_ge > topk) & finite, 1.0, 0.0)

    def index_cut():
        nbits = (seq - 1).bit_length()
        need = topk - count(lambda blk, c: blk > tau)

        def index_bit(it, p):
            cand = p | lax.shift_left(jnp.int32(1), nbits - 1 - it)
            cnt = count(lambda blk, c: (blk == tau) & (krow + c * kc < cand))
            return jnp.where(cnt < need, cand, p)
        return lax.fori_loop(0, nbits, index_bit, jnp.zeros((1, qt), jnp.int32))

    cut = lax.cond(jnp.max(excess) > 0.0, index_cut, lambda: jnp.full((1, qt), seq, jnp.int32))
    cut = jnp.where(finite, cut, -1)

    def write_bias(c, carry):
        blk = s_ref[c]
        sel = (blk > tau) | ((blk == tau) & (krow + c * kc <= cut))
        s_ref[c] = jnp.where(sel, 0.0, MASKED)
        return carry

    lax.fori_loop(0, nch, write_bias, 0)

    group = ATT_HEADS // ATT_KV_HEADS

    def chunk_operands(c):
        kgs = [k_ref[chunk_rows(c), g * ATT_HEAD_DIM:(g + 1) * ATT_HEAD_DIM] for g in range(ATT_KV_HEADS)]
        vtgs = [vt_ref[g * VT_ROWS:(g + 1) * VT_ROWS, chunk_rows(c)] for g in range(ATT_KV_HEADS)]
        bias = s_ref[c]

        def logits(h):
            return lax.dot_general(kgs[h // group], q_ref[:, h * ATT_HEAD_DIM:(h + 1) * ATT_HEAD_DIM],
                                   NT_DIMS, preferred_element_type=F32) + bias
        return logits, vtgs

    def pipelined(logits, consume):
        sts = {h: logits(h) for h in range(LOOKAHEAD)}
        for h in range(ATT_HEADS):
            p = consume(h, sts.pop(h))
            if h + LOOKAHEAD < ATT_HEADS:
                sts[h + LOOKAHEAD] = logits(h + LOOKAHEAD)
            yield h, p

    @pl.when(i == 0)
    def _():
        kf = k_ref[...].astype(F32)
        sq = kf * kf
        for g in range(ATT_KV_HEADS):
            n2 = jnp.sum(sq[:, g * ATT_HEAD_DIM:(g + 1) * ATT_HEAD_DIM], axis=1, keepdims=True)
            kmax_ref[g] = jnp.max(n2)

    qf = q_ref[...].astype(F32)
    head_of_lane = lax.broadcasted_iota(jnp.int32, (ATT_HEADS, D_MODEL), 1) // ATT_HEAD_DIM
    head_sel = jnp.where(head_of_lane == lax.broadcasted_iota(jnp.int32, (ATT_HEADS, D_MODEL), 0), 1.0, 0.0)
    qn2 = lax.dot_general(head_sel.astype(BF16), (qf * qf).astype(BF16), NT_DIMS,
                          preferred_element_type=F32)
    for h in range(ATT_HEADS):
        m_ref[h] = jnp.sqrt(qn2[h:h + 1, :] * kmax_ref[h // group]) * BOUND_SLACK

    acc_ref[...] = jnp.zeros(acc_ref.shape, F32)

    def attend_fast(c, carry):
        logits, vtgs = chunk_operands(c)
        for h, p in pipelined(logits, lambda h, st: jnp.exp2(st - m_ref[h]).astype(BF16)):
            acc_ref[h] = acc_ref[h] + jnp.dot(vtgs[h // group], p, preferred_element_type=F32)
        return carry

    lax.fori_loop(0, nch, attend_fast, 0)
    denom = _tree(jnp.minimum, [acc_ref[h, ATT_HEAD_DIM:ATT_HEAD_DIM + 1, :] for h in range(ATT_HEADS)])

    @pl.when(jnp.logical_not(jnp.min(denom) >= MIN_DENOM))
    def _():
        m_ref[...] = jnp.full(m_ref.shape, MASKED, F32)
        acc_ref[...] = jnp.zeros(acc_ref.shape, F32)

        def attend(c, carry):
            logits, vtgs = chunk_operands(c)
            alphas = {}

            def softmax_step(h, st):
                mx = _tree(jnp.maximum, [st[r * SUBLANES:(r + 1) * SUBLANES] for r in range(kc // SUBLANES)])
                m_old = m_ref[h]
                m_new = jnp.maximum(m_old, jnp.max(mx, axis=0, keepdims=True))
                m_ref[h] = m_new
                alphas[h] = jnp.exp2(m_old - m_new)
                return jnp.exp2(st - m_new).astype(BF16)

            for h, p in pipelined(logits, softmax_step):
                acc_ref[h] = alphas[h] * acc_ref[h] + jnp.dot(vtgs[h // group], p, preferred_element_type=F32)
            return carry

        lax.fori_loop(0, nch, attend, 0)

    for h in range(ATT_HEADS):
        acc = acc_ref[h]
        ot_ref[h * ATT_HEAD_DIM:(h + 1) * ATT_HEAD_DIM, :] = (
            acc[:ATT_HEAD_DIM] * (1.0 / acc[ATT_HEAD_DIM:ATT_HEAD_DIM + 1]))
    o_ref[...] = ot_ref[...].T.astype(BF16)


def _dsa_attn(q, qi, wit, k, vt, ki, batch, seq, qt):
    nq = seq // qt
    topk = min(INDEX_TOPK, seq // 4)
    qrow = lambda w: pl.BlockSpec((qt, w), lambda b, i: (b * nq + i, 0))
    kv = pl.BlockSpec((seq, LANES), lambda b, i: (b, 0))
    return pl.pallas_call(
        functools.partial(_dsa_attn_kernel, qt=qt, topk=topk, seq=seq),
        grid=(batch, nq),
        in_specs=[qrow(D_MODEL), qrow(D_MODEL),
                  pl.BlockSpec((IDX_HEADS, qt), lambda b, i: (0, b * nq + i)),
                  kv, pl.BlockSpec((ATT_KV_HEADS * VT_ROWS, seq), lambda b, i: (0, b)), kv],
        out_specs=qrow(D_MODEL),
        out_shape=jax.ShapeDtypeStruct((batch * seq, D_MODEL), BF16),
        scratch_shapes=[pltpu.VMEM((nq, qt, qt), F32),
                        pltpu.VMEM((nq, qt, qt), BF16),
                        pltpu.VMEM((ATT_HEADS, 1, qt), F32),
                        pltpu.VMEM((ATT_HEADS, VT_ROWS, qt), F32),
                        pltpu.VMEM((D_MODEL, qt), F32),
                        pltpu.SMEM((ATT_KV_HEADS,), F32)],
        compiler_params=pltpu.CompilerParams(dimension_semantics=("arbitrary", "arbitrary"),
                                             vmem_limit_bytes=VMEM_LIMIT),
        name="dsa_attn",
    )(q, qi, wit, k, vt, ki)


FF_CHUNKS = ((0, 768), (768, 768), (1536, 768), (2304, 512))


def _post_kernel(a_ref, x_ref, wo_ref, g1_ref, b1_ref, wg_ref, wu_ref, wd_ref, g2_ref, b2_ref, o_ref):
    y = jnp.dot(a_ref[...], wo_ref[...], preferred_element_type=F32)
    h = _layer_norm(DEEPNORM_ALPHA * x_ref[...] + y, g1_ref[...], b1_ref[...])
    hb = h.astype(BF16)
    f = jnp.zeros(h.shape, F32)
    for start, size in FF_CHUNKS:
        gate = jnp.dot(hb, wg_ref[:, start:start + size], preferred_element_type=F32)
        up = jnp.dot(hb, wu_ref[:, start:start + size], preferred_element_type=F32)
        act = (gate * _sigmoid(gate) * up).astype(BF16)
        f = f + jnp.dot(act, wd_ref[start:start + size, :], preferred_element_type=F32)
    o_ref[...] = _layer_norm(DEEPNORM_ALPHA * h + f, g2_ref[...], b2_ref[...])


def _post(a, x2, wo, g1, b1, wg, wu, wd, g2, b2, tm):
    n = x2.shape[0]
    row = pl.BlockSpec((tm, D_MODEL), lambda i: (i, 0))
    return pl.pallas_call(
        _post_kernel,
        grid=(n // tm,),
        in_specs=[row, row] + [_const_spec(t.shape) for t in (wo, g1, b1, wg, wu, wd, g2, b2)],
        out_specs=row,
        out_shape=jax.ShapeDtypeStruct((n, D_MODEL), F32),
        compiler_params=pltpu.CompilerParams(dimension_semantics=("arbitrary",),
                                             vmem_limit_bytes=VMEM_LIMIT),
        name="post_ffn",
    )(a, x2, wo, g1, b1, wg, wu, wd, g2, b2)


CHUNK = 64
SUB = 16
MAX_LOG2_DECAY = 100.0


def _split3(x):
    hi = x.astype(BF16)
    r1 = x - hi.astype(F32)
    mid = r1.astype(BF16)
    lo = (r1 - mid.astype(F32)).astype(BF16)
    return hi, mid, lo


def _hgrn_chunk(qs, ks, vs, lfs, sts, b_s, k_s, v_s, factored):
    heads = range(len(qs))
    rowi = lax.broadcasted_iota(jnp.int32, (CHUNK, CHUNK), 0)
    coli = lax.broadcasted_iota(jnp.int32, (CHUNK, CHUNK), 1)
    tri = jnp.where(rowi >= coli, 1.0, 0.0).astype(BF16)
    nsub = CHUNK // SUB

    bs = []
    for h in heads:
        hi, mid, lo = _split3(lfs[h])
        b = (jnp.dot(tri, hi, preferred_element_type=F32) + jnp.dot(tri, mid, preferred_element_type=F32)
             + jnp.dot(tri, lo, preferred_element_type=F32))
        bs.append(b * LOG2E)
    for h in heads:
        b_s[h] = bs[h]
        k_s[h] = ks[h]
        v_s[h] = vs[h]

    qhat = [(qs[h] * jnp.exp2(bs[h])).astype(BF16) for h in heads]
    os_ = [lax.dot_general(qhat[h], sts[h].astype(BF16), NT_DIMS, preferred_element_type=F32) for h in heads]

    new_sts = []
    b_lasts = [b_s[h, CHUNK - 1:CHUNK, :] for h in heads]
    for h in heads:
        kh = (ks[h] * jnp.exp2(b_lasts[h] - bs[h])).astype(BF16)
        new_sts.append(sts[h] * jnp.exp2(b_lasts[h])
                       + lax.dot_general(vs[h].astype(BF16), kh, TN_DIMS, preferred_element_type=F32))

    def intra_factored():
        attn = [lax.dot_general(qhat[h], (ks[h] * jnp.exp2(-bs[h])).astype(BF16), NT_DIMS,
                                preferred_element_type=F32) for h in heads]
        return [jnp.dot(jnp.where(rowi >= coli, attn[h], 0.0).astype(BF16), vs[h].astype(BF16),
                        preferred_element_type=F32) for h in heads]

    def intra_pairwise():
        outs = [jnp.zeros((CHUNK, HGRN_EXPAND), F32) for _ in heads]
        for j in range(nsub - 1):
            r0, r1 = j * SUB, (j + 1) * SUB
            attn = []
            for h in heads:
                e_j = b_s[h, r1 - 1:r1, :]
                kt = (ks[h][r0:r1] * jnp.exp2(e_j - bs[h][r0:r1])).astype(BF16)
                qt = (qs[h][r1:] * jnp.exp2(bs[h][r1:] - e_j)).astype(BF16)
                attn.append(lax.dot_general(qt, kt, NT_DIMS, preferred_element_type=F32))
            for h in heads:
                upd = jnp.dot(attn[h].astype(BF16), vs[h][r0:r1].astype(BF16), preferred_element_type=F32)
                outs[h] = jnp.concatenate([outs[h][:r1], outs[h][r1:] + upd], axis=0)
        for h in heads:
            b, q = bs[h], qs[h]
            pieces = []
            for j in range(nsub):
                r0 = j * SUB
                for half in range(SUB // SUBLANES):
                    t0 = r0 + half * SUBLANES
                    bq = b[t0:r0 + SUB]
                    qq = q[t0:r0 + SUB]
                    trow = lax.broadcasted_iota(jnp.int32, bq.shape, 0)
                    acc = jnp.zeros(bq.shape, F32)
                    for sl in range(SUBLANES):
                        s = t0 + sl
                        d = jnp.where(trow >= sl, bq - b_s[h, s:s + 1, :], -jnp.inf)
                        p = jnp.exp2(d) * (qq * k_s[h, s:s + 1, :])
                        acc = acc + jnp.sum(p, axis=1, keepdims=True) * v_s[h, s:s + 1, :]
                    if half == 0:
                        first = acc
                    else:
                        pieces.append(first[:half * SUBLANES])
                        pieces.append(first[half * SUBLANES:] + acc)
            outs[h] = outs[h] + jnp.concatenate(pieces, axis=0)
        return outs

    intra = intra_factored() if factored else intra_pairwise()
    return [os_[h] + intra[h] for h in heads], new_sts


def _hgrn_kernel(x_ref, w_ref, lb_ref, gn_ref, a_ref, st_ref, q_s, k_s, v_s, lf_s, o_s, bc_s, kc_s, vc_s, *, tc):
    @pl.when(pl.program_id(1) == 0)
    def _():
        st_ref[...] = jnp.zeros(st_ref.shape, F32)

    d = D_MODEL
    xb = x_ref[...].astype(BF16)
    lb = lb_ref[...]
    hq = jnp.dot(xb, w_ref[:, 0:d], preferred_element_type=F32)
    q = hq * _sigmoid(hq)
    hf = jnp.dot(xb, w_ref[:, d:2 * d], preferred_element_type=F32)
    fg = lb + (1.0 - lb) * _sigmoid(hf)
    k = 1.0 - fg
    lf = jnp.log(fg)
    v = jnp.dot(xb, w_ref[:, 2 * d:3 * d], preferred_element_type=F32)
    for h in range(HGRN_HEADS):
        sl = slice(h * HGRN_EXPAND, (h + 1) * HGRN_EXPAND)
        q_s[h] = q[:, sl]
        k_s[h] = k[:, sl]
        v_s[h] = v[:, sl]
        lf_s[h] = lf[:, sl]

    heads = range(HGRN_HEADS)

    def recur(factored):
        def chunk_body(ci, carry):
            rows = pl.ds(pl.multiple_of(ci * CHUNK, CHUNK), CHUNK)
            outs, sts = _hgrn_chunk([q_s[h, rows, :] for h in heads], [k_s[h, rows, :] for h in heads],
                                    [v_s[h, rows, :] for h in heads], [lf_s[h, rows, :] for h in heads],
                                    [st_ref[h] for h in heads], bc_s, kc_s, vc_s, factored)
            for h in heads:
                o_s[h, rows, :] = outs[h]
                st_ref[h] = sts[h]
            return carry

        lax.fori_loop(0, tc // CHUNK, chunk_body, 0)

    chunk_decay = [jnp.sum(lf[ci * CHUNK:(ci + 1) * CHUNK], axis=0, keepdims=True) for ci in range(tc // CHUNK)]
    mild = jnp.min(_tree(jnp.minimum, chunk_decay)) * LOG2E >= -MAX_LOG2_DECAY

    @pl.when(mild)
    def _():
        recur(True)

    @pl.when(jnp.logical_not(mild))
    def _():
        recur(False)

    hg = jnp.dot(xb, w_ref[:, 3 * d:4 * d], preferred_element_type=F32)
    gate = hg * _sigmoid(hg)
    gn = gn_ref[...]
    for h in range(HGRN_HEADS):
        sl = slice(h * HGRN_EXPAND, (h + 1) * HGRN_EXPAND)
        o = o_s[h]
        ms = jnp.mean(o * o, axis=-1, keepdims=True)
        a_ref[:, sl] = (o * lax.rsqrt(ms + RMS_EPS) * gn * gate[:, sl]).astype(BF16)


def _hgrn(x2, w_in, lb, g_norm, batch, seq, tc):
    nt = seq // tc
    row = pl.BlockSpec((tc, D_MODEL), lambda b, t: (b * nt + t, 0))
    hm = lambda: pltpu.VMEM((HGRN_HEADS, tc, HGRN_EXPAND), F32)
    cm = lambda: pltpu.VMEM((HGRN_HEADS, CHUNK, HGRN_EXPAND), F32)
    return pl.pallas_call(
        functools.partial(_hgrn_kernel, tc=tc),
        grid=(batch, nt),
        in_specs=[row, _const_spec(w_in.shape), _const_spec(lb.shape), _const_spec(g_norm.shape)],
        out_specs=row,
        out_shape=jax.ShapeDtypeStruct((batch * seq, D_MODEL), BF16),
        scratch_shapes=[pltpu.VMEM((HGRN_HEADS, HGRN_EXPAND, HGRN_EXPAND), F32),
                        hm(), hm(), hm(), hm(), hm(), cm(), cm(), cm()],
        compiler_params=pltpu.CompilerParams(dimension_semantics=("arbitrary", "arbitrary"),
                                             vmem_limit_bytes=VMEM_LIMIT),
        name="hgrn2",
    )(x2, w_in, lb, g_norm)


def _rope_tables(positions, rot_dim, period):
    half = rot_dim // 2
    inv = ROPE_THETA ** (-jnp.arange(0, rot_dim, 2, dtype=F32) / rot_dim)
    ang = positions.astype(F32)[..., None] * inv
    cos, sin = jnp.cos(ang), jnp.sin(ang)
    rest = ang.shape[:-1] + (period - 2 * half,)
    c = jnp.concatenate([cos, cos, jnp.ones(rest, F32)], axis=-1)
    s = jnp.concatenate([-sin, sin, jnp.zeros(rest, F32)], axis=-1)
    reps = LANES // period
    c = jnp.tile(c, (1, 1, reps)).reshape(-1, LANES)
    s = jnp.tile(s, (1, 1, reps)).reshape(-1, LANES)
    return c, s


def _row(v):
    return v.reshape(1, -1).astype(F32)


def _rotate_half_columns(w, half, period):
    j = np.arange(w.shape[1])
    r = j % period
    src_col = np.where(r < half, j + half, np.where(r < 2 * half, j - half, j))
    return w[:, src_col]


def _dsa_layer(x2, positions, w_in, g_cq, w_uq, w_iq, g_ik, b_ik, batch, seq, tm, qt):
    cq, sq = _rope_tables(positions, ATT_ROT_DIM, ATT_HEAD_DIM)
    ci, si = _rope_tables(positions, IDX_ROT_DIM, IDX_DIM)
    w_k = w_in[:, Q_LORA_RANK:Q_LORA_RANK + ATT_KV_DIM]
    w_in_p = jnp.concatenate([jnp.pad(w_in, ((0, 0), (0, ATT_IN_PAD - w_in.shape[1]))),
                              _rotate_half_columns(w_k, ATT_ROT_DIM // 2, ATT_HEAD_DIM)], axis=1).astype(BF16)
    q, k, vt, qi, ki, wit = _dsa_proj(
        x2, w_in_p, _row(g_cq), w_uq.astype(BF16),
        _rotate_half_columns(w_uq, ATT_ROT_DIM // 2, ATT_HEAD_DIM).astype(BF16), w_iq.astype(BF16),
        _rotate_half_columns(w_iq, IDX_ROT_DIM // 2, IDX_DIM).astype(BF16),
        _row(g_ik), _row(b_ik), cq, sq, ci, si, tm)
    return _dsa_attn(q, qi, wit, k, vt, ki, batch, seq, qt)


def kernel(x, positions, att_w_in, att_g_cq, att_w_uq, att_w_iq, att_g_ik, att_b_ik, att_w_o,
           hgrn_lb_logits, hgrn_w_in, hgrn_g_norm, hgrn_w_o,
           ffn_w_gate, ffn_w_up, ffn_w_down, ln_g, ln_b):
    batch, seq, d = x.shape
    n = batch * seq
    tm = min(512, n)
    qt = min(256, seq)
    tc = min(256, seq)
    lb_all = jnp.cumsum(jax.nn.softmax(hgrn_lb_logits.astype(F32), axis=0), axis=0)
    lb_all = lb_all - lb_all[0]

    h = x.reshape(n, d)
    for layer in range(DEPTH):
        j = layer // 2
        if layer % 2 == 0:
            a = _dsa_layer(h, positions, att_w_in[j], att_g_cq[j], att_w_uq[j], att_w_iq[j],
                           att_g_ik[j], att_b_ik[j], batch, seq, tm, qt)
            w_o = att_w_o[j]
        else:
            a = _hgrn(h, hgrn_w_in[j].astype(BF16), _row(lb_all[layer]), _row(hgrn_g_norm[j]),
                      batch, seq, tc)
            w_o = hgrn_w_o[j]
        h = _post(a, h, w_o.astype(BF16), _row(ln_g[layer, 0]), _row(ln_b[layer, 0]),
                  ffn_w_gate[layer].astype(BF16), ffn_w_up[layer].astype(BF16),
                  ffn_w_down[layer].astype(BF16), _row(ln_g[layer, 1]), _row(ln_b[layer, 1]), tm)
    return h.reshape(batch, seq, d)
```
